```python
import math
import jax, jax.numpy as jnp
from jax import lax
import numpy as np

D_MODEL = 1024
BATCH = 16
SEQ = 4096
DEPTH = 1
DEC_BATCH = 1
DEC_SEQ = 16384
PAST_LEN = 128

GLA_HEADS = 4
GLA_DK = 128
GLA_DV = 256
GLA_GATE_RANK = 16
GLA_GATE_NORM = 16.0
GLA_CHUNK = 64
MLA_HEADS = 8
MLA_Q_RANK = 256
MLA_KV_RANK = 128
MLA_NOPE = 128
MLA_ROPE = 64
MLA_V = 128
MLA_QBLOCK = 128
ROPE_THETA = 10000.0
EPS = 1e-6

GLA_QK_W = GLA_HEADS * GLA_DK
GLA_V_W = GLA_HEADS * GLA_DV
MLA_V_W = MLA_HEADS * MLA_V
SPLIT_SIZES = (GLA_QK_W, GLA_QK_W, GLA_V_W, GLA_GATE_RANK, GLA_GATE_RANK, GLA_V_W,
               MLA_Q_RANK, MLA_KV_RANK, MLA_ROPE, MLA_V_W, D_MODEL, D_MODEL)
D_IN = GLA_QK_W * 2 + GLA_V_W * 2 + GLA_GATE_RANK * 2 + MLA_Q_RANK + MLA_KV_RANK + MLA_ROPE + MLA_V_W + 2 * D_MODEL

kernel_name = 'hybrid_gla_mla_gated_encoder'


def rmsnorm(x, g):
    xf = x.astype(jnp.float32)
    y = xf * lax.rsqrt(jnp.mean(xf * xf, axis=-1, keepdims=True) + EPS)
    return (y * g.astype(jnp.float32)).astype(x.dtype)


def split_cols(t, sizes):
    idx = []
    acc = 0
    for s in sizes[:-1]:
        acc += s
        idx.append(acc)
    return jnp.split(t, idx, axis=-1)


def apply_rope(x, pos):
    half = MLA_ROPE // 2
    freqs = ROPE_THETA ** (-jnp.arange(half, dtype=jnp.float32) / half)
    ang = pos.astype(jnp.float32)[:, None] * freqs[None, :]
    cos = jnp.cos(ang)[:, None, :]
    sin = jnp.sin(ang)[:, None, :]
    xf = x.astype(jnp.float32)
    x1, x2 = xf[..., :half], xf[..., half:]
    out = jnp.concatenate([x1 * cos - x2 * sin, x2 * cos + x1 * sin], axis=-1)
    return out.astype(x.dtype)


def gla_chunked(q, k, v, g, strict):
    B, H, S, dk = q.shape
    C = GLA_CHUNK
    N = S // C

    def to_chunks(t):
        return t.reshape(B, H, N, C, t.shape[-1]).transpose(2, 0, 1, 3, 4)

    qc, kc, vc, gc = to_chunks(q), to_chunks(k), to_chunks(v), to_chunks(g)
    bc = jnp.cumsum(gc, axis=3)
    t_idx = jnp.arange(C)
    if strict:
        mask = t_idx[:, None] > t_idx[None, :]
    else:
        mask = t_idx[:, None] >= t_idx[None, :]

    def step(state, inp):
        qi, ki, vi, bi = inp
        diff = bi[:, :, :, None, :] - bi[:, :, None, :, :]
        decay = jnp.exp(jnp.where(mask[None, None, :, :, None], diff, -jnp.inf))
        scores = jnp.einsum('bhtd,bhsd,bhtsd->bhts', qi, ki, decay)
        o = jnp.einsum('bhts,bhsv->bhtv', scores, vi) + jnp.einsum('bhtd,bhdv->bhtv', qi * jnp.exp(bi), state)
        b_last = bi[:, :, -1:, :]
        k_dec = ki * jnp.exp(b_last - bi)
        state = state * jnp.exp(b_last[:, :, 0, :])[..., None] + jnp.einsum('bhsd,bhsv->bhdv', k_dec, vi)
        return state, o

    state0 = jnp.zeros((B, H, dk, v.shape[-1]), jnp.float32)
    _, oc = lax.scan(step, state0, (qc, kc, vc, bc))
    return oc.transpose(1, 2, 0, 3, 4).reshape(B, H, S, v.shape[-1])


def gla_branch(q_a, k_a, v_a, gr_f, gr_b, wg2_f, bg2_f, wg2_b, bg2_b, norm_g):
    B, S, _ = q_a.shape

    def heads(t, d):
        return t.reshape(B, S, GLA_HEADS, d).transpose(0, 2, 1, 3).astype(jnp.float32)

    q = heads(q_a, GLA_DK) * (GLA_DK ** -0.5)
    k = heads(k_a, GLA_DK)
    v = heads(v_a, GLA_DV)
    g_f = heads(jax.nn.log_sigmoid((gr_f @ wg2_f + bg2_f).astype(jnp.float32)) / GLA_GATE_NORM, GLA_DK)
    g_b = heads(jax.nn.log_sigmoid((gr_b @ wg2_b + bg2_b).astype(jnp.float32)) / GLA_GATE_NORM, GLA_DK)
    flip = lambda t: jnp.flip(t, axis=2)
    o_fwd = gla_chunked(q, k, v, g_f, strict=False)
    o_bwd = flip(gla_chunked(flip(q), flip(k), flip(v), flip(g_b), strict=True))
    o = rmsnorm(o_fwd + o_bwd, norm_g)
    return o.transpose(0, 2, 1, 3).reshape(B, S, GLA_V_W).astype(q_a.dtype)


def mla_branch(cq, ckv, k_r, q_norm_g, w_uq, kv_norm_g, w_ukv):
    B, S, _ = cq.shape
    pos = jnp.arange(S)
    c_q = rmsnorm(cq, q_norm_g)
    qf = (c_q @ w_uq).reshape(B, S, MLA_HEADS, MLA_NOPE + MLA_ROPE)
    q_nope = qf[..., :MLA_NOPE]
    q_pe = apply_rope(qf[..., MLA_NOPE:], pos)
    c_kv = rmsnorm(ckv, kv_norm_g)
    kv = (c_kv @ w_ukv).reshape(B, S, MLA_HEADS, MLA_NOPE + MLA_V)
    k_nope = kv[..., :MLA_NOPE]
    v = kv[..., MLA_NOPE:]
    k_pe = apply_rope(k_r[:, :, None, :], pos)[:, :, 0, :]
    scale = (MLA_NOPE + MLA_ROPE) ** -0.5
    NB = S // MLA_QBLOCK

    def blocks(t):
        return t.reshape(B, NB, MLA_QBLOCK, MLA_HEADS, t.shape[-1]).transpose(1, 0, 2, 3, 4)

    def attend(qb):
        qn, qp = qb
        s = (jnp.einsum('bqhd,bkhd->bhqk', qn, k_nope) + jnp.einsum('bqhr,bkr->bhqk', qp, k_pe)) * scale
        p = jax.nn.softmax(s.astype(jnp.float32), axis=-1).astype(v.dtype)
        return jnp.einsum('bhqk,bkhv->bqhv', p, v)

    ob = lax.map(attend, (blocks(q_nope), blocks(q_pe)))
    return ob.transpose(1, 0, 2, 3, 4).reshape(B, S, MLA_V_W)


def hybrid_layer(x, pre_g, w_in, wg2_f, bg2_f, wg2_b, bg2_b, gla_norm_g, q_norm_g, w_uq,
                 kv_norm_g, w_ukv, w_branch_a, w_branch_b, w_out, post_g):
    h = rmsnorm(x, pre_g)
    proj = h @ w_in
    (q_a, k_a, v_a, gr_f, gr_b, z_a, cq, ckv, k_r, z_b, gate_a, gate_b) = split_cols(proj, SPLIT_SIZES)
    o_a = gla_branch(q_a, k_a, v_a, gr_f, gr_b, wg2_f, bg2_f, wg2_b, bg2_b, gla_norm_g)
    o_b = mla_branch(cq, ckv, k_r, q_norm_g, w_uq, kv_norm_g, w_ukv)
    y_a = o_a * jax.nn.silu(z_a)
    y_b = o_b * jax.nn.silu(z_b)
    merged = jax.nn.sigmoid(gate_a) * (y_a @ w_branch_a) + jax.nn.sigmoid(gate_b) * (y_b @ w_branch_b)
    out = merged @ w_out
    return x + rmsnorm(out, post_g)


def setup_inputs(seed: int = 0) -> dict:
    key = jax.random.key(seed)
    ks = jax.random.split(key, 20)
    L = DEPTH
    nrm = lambda k, shape, fan: jax.random.normal(k, shape, jnp.float32) * (fan ** -0.5)
    gain = lambda k, shape: 1.0 + 0.05 * jax.random.normal(k, shape, jnp.float32)
    return {
        'x_prompt': jax.random.normal(ks[0], (BATCH, SEQ, D_MODEL), jnp.float32),
        'x_sample': jax.random.normal(ks[1], (DEC_BATCH, DEC_SEQ, D_MODEL), jnp.float32),
        'pre_norm_g': gain(ks[2], (L, D_MODEL)),
        'w_in': nrm(ks[3], (L, D_MODEL, D_IN), D_MODEL),
        'gla_wg2_fwd': nrm(ks[4], (L, GLA_GATE_RANK, GLA_QK_W), GLA_GATE_RANK),
        'gla_bg2_fwd': 0.1 * jax.random.normal(ks[5], (L, GLA_QK_W), jnp.float32),
        'gla_wg2_bwd': nrm(ks[6], (L, GLA_GATE_RANK, GLA_QK_W), GLA_GATE_RANK),
        'gla_bg2_bwd': 0.1 * jax.random.normal(ks[7], (L, GLA_QK_W), jnp.float32),
        'gla_out_norm_g': gain(ks[8], (L, GLA_DV)),
        'mla_q_norm_g': gain(ks[9], (L, MLA_Q_RANK)),
        'mla_w_uq': nrm(ks[10], (L, MLA_Q_RANK, MLA_HEADS * (MLA_NOPE + MLA_ROPE)), MLA_Q_RANK),
        'mla_kv_norm_g': gain(ks[11], (L, MLA_KV_RANK)),
        'mla_w_ukv': nrm(ks[12], (L, MLA_KV_RANK, MLA_HEADS * (MLA_NOPE + MLA_V)), MLA_KV_RANK),
        'w_branch_a': nrm(ks[13], (L, GLA_V_W, D_MODEL), GLA_V_W),
        'w_branch_b': nrm(ks[14], (L, MLA_V_W, D_MODEL), MLA_V_W),
        'w_out': nrm(ks[15], (L, D_MODEL, D_MODEL), D_MODEL),
        'post_norm_g': gain(ks[16], (L, D_MODEL)),
    }


def reference(x_prompt, x_sample, pre_norm_g, w_in, gla_wg2_fwd, gla_bg2_fwd, gla_wg2_bwd, gla_bg2_bwd,
              gla_out_norm_g, mla_q_norm_g, mla_w_uq, mla_kv_norm_g, mla_w_ukv, w_branch_a, w_branch_b,
              w_out, post_norm_g):
    y_prompt = x_prompt
    y_sample = x_sample
    for l in range(DEPTH):
        p = (pre_norm_g[l], w_in[l], gla_wg2_fwd[l], gla_bg2_fwd[l], gla_wg2_bwd[l], gla_bg2_bwd[l],
             gla_out_norm_g[l], mla_q_norm_g[l], mla_w_uq[l], mla_kv_norm_g[l], mla_w_ukv[l],
             w_branch_a[l], w_branch_b[l], w_out[l], post_norm_g[l])
        y_prompt = hybrid_layer(y_prompt, *p)
        y_sample = hybrid_layer(y_sample, *p)
    return (y_prompt, y_sample)
```

```python
import functools
import math

import jax
import jax.numpy as jnp
from jax import lax
from jax.experimental import pallas as pl
from jax.experimental.pallas import tpu as pltpu

D_MODEL = 1024
GLA_HEADS = 4
GLA_DK = 128
GLA_DV = 256
GLA_GATE_RANK = 16
GLA_GATE_NORM = 16.0
MLA_HEADS = 8
MLA_Q_RANK = 256
MLA_KV_RANK = 128
MLA_NOPE = 128
MLA_ROPE = 64
MLA_V = 128
ROPE_THETA = 10000.0
EPS = 1e-6

GLA_QK_W = GLA_HEADS * GLA_DK
GLA_V_W = GLA_HEADS * GLA_DV
MLA_V_W = MLA_HEADS * MLA_V
MLA_QK = MLA_NOPE + MLA_ROPE

V7X_LANES = 128
V7X_VMEM_LIMIT_BYTES = 56 * 1024 * 1024

QKV_W = 2 * GLA_QK_W + GLA_V_W
LAT_CQ = 0
LAT_CKV = MLA_Q_RANK
LAT_KR = LAT_CKV + MLA_KV_RANK
LAT_KROT = LAT_KR + V7X_LANES
LAT_GR = LAT_KROT + V7X_LANES
LAT_W = LAT_GR + V7X_LANES
ACT_W = 4 * D_MODEL

GLA_KERNEL_CHUNK = 128
GLA_EXP_CLAMP = 80.0
LOG2E = 1.4426950408889634


def _cparams(semantics):
    return pltpu.CompilerParams(dimension_semantics=semantics,
                                vmem_limit_bytes=V7X_VMEM_LIMIT_BYTES)


def _const_spec(shape):
    return pl.BlockSpec(shape, lambda *_: (0,) * len(shape))


def _rms(x, g):
    return x * lax.rsqrt(jnp.mean(x * x, axis=-1, keepdims=True) + EPS) * g


def _sigmoid(x):
    return 1.0 / (1.0 + jnp.exp(-x))


def _bdot(a, b):
    return jnp.dot(a, b, preferred_element_type=jnp.float32)


def _bdot_nt(a, b):
    return lax.dot_general(a, b, (((1,), (1,)), ((), ())), preferred_element_type=jnp.float32)


def _bdot_tn(a, b):
    return lax.dot_general(a, b, (((0,), (0,)), ((), ())), preferred_element_type=jnp.float32)


def _proj_kernel(x_ref, g_ref, wqkv_ref, wlat_ref, wact_ref, qkv_ref, lat_ref, act_ref):
    h = _rms(x_ref[...], g_ref[...]).astype(jnp.bfloat16)
    qkv = _bdot(h, wqkv_ref[...])
    qkv_ref[:, :GLA_QK_W] = (qkv[:, :GLA_QK_W] * (GLA_DK ** -0.5)).astype(jnp.bfloat16)
    qkv_ref[:, GLA_QK_W:] = qkv[:, GLA_QK_W:].astype(jnp.bfloat16)
    lat_ref[...] = _bdot(h, wlat_ref[...])
    z = _bdot(h, wact_ref[:, :2 * D_MODEL])
    act_ref[:, :2 * D_MODEL] = (z * _sigmoid(z)).astype(jnp.bfloat16)
    gate = _bdot(h, wact_ref[:, 2 * D_MODEL:])
    act_ref[:, 2 * D_MODEL:] = _sigmoid(gate).astype(jnp.bfloat16)


def _proj(x2d, pre_g, wqkv, wlat, wact, ts):
    t = x2d.shape[0]
    return pl.pallas_call(
        _proj_kernel,
        grid=(t // ts,),
        in_specs=[pl.BlockSpec((ts, D_MODEL), lambda i: (i, 0)),
                  _const_spec((1, D_MODEL)),
                  _const_spec((D_MODEL, QKV_W)),
                  _const_spec((D_MODEL, LAT_W)),
                  _const_spec((D_MODEL, ACT_W))],
        out_specs=[pl.BlockSpec((ts, QKV_W), lambda i: (i, 0)),
                   pl.BlockSpec((ts, LAT_W), lambda i: (i, 0)),
                   pl.BlockSpec((ts, ACT_W), lambda i: (i, 0))],
        out_shape=[jax.ShapeDtypeStruct((t, QKV_W), jnp.bfloat16),
                   jax.ShapeDtypeStruct((t, LAT_W), jnp.float32),
                   jax.ShapeDtypeStruct((t, ACT_W), jnp.bfloat16)],
        compiler_params=_cparams(("parallel",)),
        name="proj",
    )(x2d, pre_g, wqkv, wlat, wact)


def _mla_up_kernel(lat_ref, cos_ref, sin_ref, qg_ref, kvg_ref, wqn_ref, wqr_ref, wqrot_ref,
                   wkn_ref, wv_ref, qt_ref, kf_ref, vt_ref):
    lat = lat_ref[...]
    cos = cos_ref[...]
    sin = sin_ref[...]
    c_q = _rms(lat[:, LAT_CQ:LAT_CQ + MLA_Q_RANK], qg_ref[...]).astype(jnp.bfloat16)
    c_kv = _rms(lat[:, LAT_CKV:LAT_CKV + MLA_KV_RANK], kvg_ref[...]).astype(jnp.bfloat16)

    q_scale = (MLA_QK ** -0.5) * LOG2E
    qn_t = (_bdot(c_q, wqn_ref[...]) * q_scale).T
    cos_h = jnp.concatenate([cos, cos], axis=1)
    sin_h = jnp.concatenate([sin, sin], axis=1)
    cos_q = jnp.concatenate([cos_h] * (MLA_HEADS // 2), axis=1)
    sin_q = jnp.concatenate([sin_h] * (MLA_HEADS // 2), axis=1)
    q_pe = _bdot(c_q, wqr_ref[...]) * cos_q + _bdot(c_q, wqrot_ref[...]) * sin_q
    qpe_t = (q_pe * q_scale).T

    k_pe = (lat[:, LAT_KR:LAT_KR + MLA_ROPE] * cos
            + lat[:, LAT_KROT:LAT_KROT + MLA_ROPE] * sin).astype(jnp.bfloat16)
    kn = _bdot(c_kv, wkn_ref[...]).astype(jnp.bfloat16)
    v_t = _bdot(c_kv, wv_ref[...]).T

    for h in range(MLA_HEADS):
        qt_ref[h, :MLA_NOPE, :] = qn_t[h * MLA_NOPE:(h + 1) * MLA_NOPE, :].astype(jnp.bfloat16)
        qt_ref[h, MLA_NOPE:, :] = qpe_t[h * MLA_ROPE:(h + 1) * MLA_ROPE, :].astype(jnp.bfloat16)
        kf_ref[h, :, :MLA_NOPE] = kn[:, h * MLA_NOPE:(h + 1) * MLA_NOPE]
        kf_ref[h, :, MLA_NOPE:] = k_pe
        vt_ref[h, :, :] = v_t[h * MLA_V:(h + 1) * MLA_V, :].astype(jnp.bfloat16)


def _mla_up(lat3, cos, sin, qg, kvg, wqn, wqr, wqrot, wkn, wv, ts):
    b, s, _ = lat3.shape
    hh = MLA_HEADS
    return pl.pallas_call(
        _mla_up_kernel,
        grid=(b, s // ts),
        in_specs=[pl.BlockSpec((None, ts, LAT_W), lambda i, j: (i, j, 0)),
                  pl.BlockSpec((ts, MLA_ROPE), lambda i, j: (j, 0)),
                  pl.BlockSpec((ts, MLA_ROPE), lambda i, j: (j, 0)),
                  _const_spec((1, MLA_Q_RANK)),
                  _const_spec((1, MLA_KV_RANK)),
                  _const_spec((MLA_Q_RANK, hh * MLA_NOPE)),
                  _const_spec((MLA_Q_RANK, hh * MLA_ROPE)),
                  _const_spec((MLA_Q_RANK, hh * MLA_ROPE)),
                  _const_spec((MLA_KV_RANK, hh * MLA_NOPE)),
                  _const_spec((MLA_KV_RANK, hh * MLA_V))],
        out_specs=[pl.BlockSpec((None, hh, MLA_QK, ts), lambda i, j: (i, 0, 0, j)),
                   pl.BlockSpec((None, hh, ts, MLA_QK), lambda i, j: (i, 0, j, 0)),
                   pl.BlockSpec((None, hh, MLA_V, ts), lambda i, j: (i, 0, 0, j))],
        out_shape=[jax.ShapeDtypeStruct((b, hh, MLA_QK, s), jnp.bfloat16),
                   jax.ShapeDtypeStruct((b, hh, s, MLA_QK), jnp.bfloat16),
                   jax.ShapeDtypeStruct((b, hh, MLA_V, s), jnp.bfloat16)],
        compiler_params=_cparams(("parallel", "parallel")),
        name="mla_up",
    )(lat3, cos, sin, qg, kvg, wqn, wqr, wqrot, wkn, wv)


def _attn_kernel(qt_ref, k_ref, vt_ref, o_ref, m_ref, l_ref, acc_ref):
    ki = pl.program_id(3)

    @pl.when(ki == 0)
    def _():
        m_ref[...] = jnp.full_like(m_ref, -jnp.inf)
        l_ref[...] = jnp.zeros_like(l_ref)
        acc_ref[...] = jnp.zeros_like(acc_ref)

    s = _bdot(k_ref[...], qt_ref[...])
    m_old = m_ref[...]
    m_new = jnp.maximum(m_old, jnp.max(s, axis=0, keepdims=True))
    alpha = jnp.exp2(m_old - m_new)
    p = jnp.exp2(s - m_new)
    l_ref[...] = alpha * l_ref[...] + jnp.sum(p, axis=0, keepdims=True)
    acc_ref[...] = alpha * acc_ref[...] + _bdot(vt_ref[...], p.astype(jnp.bfloat16))
    m_ref[...] = m_new

    @pl.when(ki == pl.num_programs(3) - 1)
    def _():
        o_ref[...] = (acc_ref[...] / l_ref[...]).T.astype(o_ref.dtype)


def _attn(qt, kf, vt, tq, tk):
    b, hh, _, s = qt.shape
    return pl.pallas_call(
        _attn_kernel,
        grid=(b, hh, s // tq, s // tk),
        in_specs=[pl.BlockSpec((None, None, MLA_QK, tq), lambda i, h, q, k: (i, h, 0, q)),
                  pl.BlockSpec((None, None, tk, MLA_QK), lambda i, h, q, k: (i, h, k, 0)),
                  pl.BlockSpec((None, None, MLA_V, tk), lambda i, h, q, k: (i, h, 0, k))],
        out_specs=pl.BlockSpec((None, tq, MLA_V), lambda i, h, q, k: (i, q, h)),
        out_shape=jax.ShapeDtypeStruct((b, s, MLA_V_W), jnp.bfloat16),
        scratch_shapes=[pltpu.VMEM((1, tq), jnp.float32),
                        pltpu.VMEM((1, tq), jnp.float32),
                        pltpu.VMEM((MLA_V, tq), jnp.float32)],
        compiler_params=_cparams(("parallel", "parallel", "parallel", "arbitrary")),
        name="attn",
    )(qt, kf, vt)


def _split3(x):
    hi = x.astype(jnp.bfloat16)
    r1 = x - hi.astype(jnp.float32)
    mid = r1.astype(jnp.bfloat16)
    lo = (r1 - mid.astype(jnp.float32)).astype(jnp.bfloat16)
    return hi, mid, lo


def _gla_direction(qkv_ref, lat_ref, wg_ref, bg_ref, o_ref, state_ref, reverse):
    c = GLA_KERNEL_CHUNK
    tb = qkv_ref.shape[0]
    n_chunks = tb // c
    row = lax.broadcasted_iota(jnp.int32, (c, c), 0)
    col = lax.broadcasted_iota(jnp.int32, (c, c), 1)
    if reverse:
        tri = (col >= row).astype(jnp.bfloat16)
        keep = col > row
        last = 0
    else:
        tri = (col <= row).astype(jnp.bfloat16)
        keep = col <= row
        last = c - 1
    mid = c // 2

    def chunk_body(ci, carry):
        cidx = (n_chunks - 1 - ci) if reverse else ci
        r0 = pl.multiple_of(cidx * c, c)
        rows = pl.ds(r0, c)
        x = _bdot(lat_ref[rows, :].astype(jnp.bfloat16), wg_ref[...]) + bg_ref[...]
        g = (jnp.minimum(x, 0.0) - jnp.log1p(jnp.exp(-jnp.abs(x)))) * (1.0 / GLA_GATE_NORM)
        g_hi, g_mid, g_lo = _split3(g)
        b = _bdot(tri, g_hi) + _bdot(tri, g_mid) + _bdot(tri, g_lo)
        b_mid = b[mid:mid + 1, :]
        b_last = b[last:last + 1, :]
        d = b - b_mid
        e_q = jnp.exp(jnp.minimum(d, GLA_EXP_CLAMP))
        e_k = jnp.exp(jnp.minimum(-d, GLA_EXP_CLAMP))
        q = qkv_ref[rows, 0:GLA_QK_W].astype(jnp.float32)
        k = qkv_ref[rows, GLA_QK_W:2 * GLA_QK_W].astype(jnp.float32)
        q_in = (q * e_q).astype(jnp.bfloat16)
        k_in = (k * e_k).astype(jnp.bfloat16)
        q_st = (q * jnp.exp(b)).astype(jnp.bfloat16)
        k_st = (k * jnp.exp(b_last - b)).astype(jnp.bfloat16)
        e_last = jnp.exp(b_last)
        for h in range(GLA_HEADS):
            ks = slice(h * GLA_DK, (h + 1) * GLA_DK)
            v = qkv_ref[rows, 2 * GLA_QK_W + h * GLA_DV:2 * GLA_QK_W + (h + 1) * GLA_DV]
            a = _bdot_nt(q_in[:, ks], k_in[:, ks])
            a = jnp.where(keep, a, 0.0).astype(jnp.bfloat16)
            st = state_ref[h]
            o = _bdot(a, v) + _bdot(q_st[:, ks], st.astype(jnp.bfloat16))
            o_ref[rows, h * GLA_DV:(h + 1) * GLA_DV] = o.astype(o_ref.dtype)
            upd = _bdot_tn(k_st[:, ks], v)
            decay_col = jnp.broadcast_to(e_last[:, ks], (GLA_DK, GLA_DK)).T
            decay = jnp.concatenate([decay_col] * (GLA_DV // GLA_DK), axis=1)
            state_ref[h] = st * decay + upd
        return carry

    lax.fori_loop(0, n_chunks, chunk_body, 0)


def _gla_kernel(qkv_f_ref, lat_f_ref, qkv_b_ref, lat_b_ref, wgf_ref, bgf_ref, wgb_ref, bgb_ref,
                of_ref, ob_ref, sf_ref, sb_ref):
    @pl.when(pl.program_id(1) == 0)
    def _():
        sf_ref[...] = jnp.zeros_like(sf_ref)
        sb_ref[...] = jnp.zeros_like(sb_ref)

    _gla_direction(qkv_f_ref, lat_f_ref, wgf_ref, bgf_ref, of_ref, sf_ref, reverse=False)
    _gla_direction(qkv_b_ref, lat_b_ref, wgb_ref, bgb_ref, ob_ref, sb_ref, reverse=True)


def _gla(qkv3, lat3, wgf, bgf, wgb, bgb, tb):
    b, s, _ = qkv3.shape
    nb = s // tb
    gr_blk = LAT_GR // V7X_LANES
    fwd = lambda i, j: (i, j, 0)
    bwd = lambda i, j: (i, nb - 1 - j, 0)
    return pl.pallas_call(
        _gla_kernel,
        grid=(b, nb),
        in_specs=[pl.BlockSpec((None, tb, QKV_W), fwd),
                  pl.BlockSpec((None, tb, V7X_LANES), lambda i, j: (i, j, gr_blk)),
                  pl.BlockSpec((None, tb, QKV_W), bwd),
                  pl.BlockSpec((None, tb, V7X_LANES), lambda i, j: (i, nb - 1 - j, gr_blk)),
                  _const_spec((V7X_LANES, GLA_QK_W)),
                  _const_spec((1, GLA_QK_W)),
                  _const_spec((V7X_LANES, GLA_QK_W)),
                  _const_spec((1, GLA_QK_W))],
        out_specs=[pl.BlockSpec((None, tb, GLA_V_W), fwd),
                   pl.BlockSpec((None, tb, GLA_V_W), bwd)],
        out_shape=[jax.ShapeDtypeStruct((b, s, GLA_V_W), jnp.bfloat16),
                   jax.ShapeDtypeStruct((b, s, GLA_V_W), jnp.bfloat16)],
        scratch_shapes=[pltpu.VMEM((GLA_HEADS, GLA_DK, GLA_DV), jnp.float32),
                        pltpu.VMEM((GLA_HEADS, GLA_DK, GLA_DV), jnp.float32)],
        compiler_params=_cparams(("parallel", "arbitrary")),
        name="gla",
    )(qkv3, lat3, qkv3, lat3, wgf, bgf, wgb, bgb)


def _final_kernel(x_ref, of_ref, ob_ref, om_ref, act_ref, gg_ref, pg_ref, wa_ref, wb_ref, wo_ref,
                  y_ref):
    o_sum = of_ref[...].astype(jnp.float32) + ob_ref[...].astype(jnp.float32)
    gg = gg_ref[...]
    o_a = jnp.concatenate(
        [_rms(o_sum[:, h * GLA_DV:(h + 1) * GLA_DV], gg) for h in range(GLA_HEADS)], axis=1)
    y_a = (o_a * act_ref[:, 0:D_MODEL].astype(jnp.float32)).astype(jnp.bfloat16)
    y_b = om_ref[...] * act_ref[:, D_MODEL:2 * D_MODEL]
    merged = (act_ref[:, 2 * D_MODEL:3 * D_MODEL].astype(jnp.float32) * _bdot(y_a, wa_ref[...])
              + act_ref[:, 3 * D_MODEL:].astype(jnp.float32) * _bdot(y_b, wb_ref[...]))
    out = _bdot(merged.astype(jnp.bfloat16), wo_ref[...])
    y_ref[...] = x_ref[...] + _rms(out, pg_ref[...])


def _final(x2d, o_f, o_b, o_m, act, gg, pg, wa, wb, wo, ts):
    t = x2d.shape[0]
    tok = lambda w: pl.BlockSpec((ts, w), lambda i: (i, 0))
    return pl.pallas_call(
        _final_kernel,
        grid=(t // ts,),
        in_specs=[tok(D_MODEL), tok(GLA_V_W), tok(GLA_V_W), tok(MLA_V_W), tok(ACT_W),
                  _const_spec((1, GLA_DV)), _const_spec((1, D_MODEL)),
                  _const_spec((GLA_V_W, D_MODEL)), _const_spec((MLA_V_W, D_MODEL)),
                  _const_spec((D_MODEL, D_MODEL))],
        out_specs=tok(D_MODEL),
        out_shape=jax.ShapeDtypeStruct((t, D_MODEL), jnp.float32),
        compiler_params=_cparams(("parallel",)),
        name="final",
    )(x2d, o_f, o_b, o_m, act, gg, pg, wa, wb, wo)


def _rotate_half_cols(w):
    half = MLA_ROPE // 2
    w3 = w.reshape(w.shape[0], -1, MLA_ROPE)
    return jnp.concatenate([-w3[..., half:], w3[..., :half]], axis=-1).reshape(w.shape)


def _pad_cols(w, width):
    return jnp.pad(w, ((0, 0), (0, width - w.shape[1])))


def _prepare_weights(pre_g, w_in, wg2_f, bg2_f, wg2_b, bg2_b, gla_g, q_g, w_uq, kv_g, w_ukv,
                     w_a, w_b, w_o, post_g):
    bf = jnp.bfloat16
    sizes = (GLA_QK_W, GLA_QK_W, GLA_V_W, GLA_GATE_RANK, GLA_GATE_RANK, GLA_V_W,
             MLA_Q_RANK, MLA_KV_RANK, MLA_ROPE, MLA_V_W, D_MODEL, D_MODEL)
    offs = [0]
    for sz in sizes:
        offs.append(offs[-1] + sz)
    (w_q, w_k, w_v, w_grf, w_grb, w_za, w_cq, w_ckv, w_kr, w_zb, w_ga, w_gb) = [
        w_in[:, offs[i]:offs[i + 1]] for i in range(len(sizes))]
    wqkv = jnp.concatenate([w_q, w_k, w_v], axis=1).astype(bf)
    wlat = jnp.concatenate([
        w_cq, w_ckv, _pad_cols(w_kr, V7X_LANES), _pad_cols(_rotate_half_cols(w_kr), V7X_LANES),
        _pad_cols(jnp.concatenate([w_grf, w_grb], axis=1), V7X_LANES)], axis=1).astype(bf)
    wact = jnp.concatenate([w_za, w_zb, w_ga, w_gb], axis=1).astype(bf)

    zeros_gate = jnp.zeros((V7X_LANES, GLA_QK_W), jnp.float32)
    wgf = zeros_gate.at[0:GLA_GATE_RANK].set(wg2_f).astype(bf)
    wgb = zeros_gate.at[GLA_GATE_RANK:2 * GLA_GATE_RANK].set(wg2_b).astype(bf)

    uq = w_uq.reshape(MLA_Q_RANK, MLA_HEADS, MLA_QK)
    wqn = uq[:, :, :MLA_NOPE].reshape(MLA_Q_RANK, MLA_HEADS * MLA_NOPE)
    wqr = uq[:, :, MLA_NOPE:].reshape(MLA_Q_RANK, MLA_HEADS * MLA_ROPE)
    ukv = w_ukv.reshape(MLA_KV_RANK, MLA_HEADS, MLA_NOPE + MLA_V)
    wkn = ukv[:, :, :MLA_NOPE].reshape(MLA_KV_RANK, MLA_HEADS * MLA_NOPE)
    wv = ukv[:, :, MLA_NOPE:].reshape(MLA_KV_RANK, MLA_HEADS * MLA_V)
    return dict(
        pre_g=pre_g[None, :], wqkv=wqkv, wlat=wlat, wact=wact,
        wgf=wgf, bgf=bg2_f[None, :], wgb=wgb, bgb=bg2_b[None, :],
        gla_g=gla_g[None, :], q_g=q_g[None, :], kv_g=kv_g[None, :],
        wqn=wqn.astype(bf), wqr=wqr.astype(bf), wqrot=_rotate_half_cols(wqr).astype(bf),
        wkn=wkn.astype(bf), wv=wv.astype(bf),
        w_a=w_a.astype(bf), w_b=w_b.astype(bf), w_o=w_o.astype(bf), post_g=post_g[None, :])


def _rope_tables(s):
    half = MLA_ROPE // 2
    freqs = ROPE_THETA ** (-jnp.arange(half, dtype=jnp.float32) / half)
    ang = jnp.arange(s).astype(jnp.float32)[:, None] * freqs[None, :]
    cos = jnp.cos(ang)
    sin = jnp.sin(ang)
    return jnp.concatenate([cos, cos], axis=1), jnp.concatenate([sin, sin], axis=1)


def _tiles(b, s):
    t = b * s
    return dict(ts=min(512, t), ts_up=min(512, s), tb=min(512, s),
                tq=min(1024, s), tk=min(512, s))


def _layer(x, w):
    b, s, _ = x.shape
    t = b * s
    tl = _tiles(b, s)
    x2d = x.reshape(t, D_MODEL)
    qkv, lat, act = _proj(x2d, w["pre_g"], w["wqkv"], w["wlat"], w["wact"], tl["ts"])
    lat3 = lat.reshape(b, s, LAT_W)
    cos, sin = _rope_tables(s)
    qt, kf, vt = _mla_up(lat3, cos, sin, w["q_g"], w["kv_g"], w["wqn"], w["wqr"], w["wqrot"],
                         w["wkn"], w["wv"], tl["ts_up"])
    o_m = _attn(qt, kf, vt, tl["tq"], tl["tk"])
    o_f, o_b = _gla(qkv.reshape(b, s, QKV_W), lat3, w["wgf"], w["bgf"], w["wgb"], w["bgb"],
                    tl["tb"])
    y = _final(x2d, o_f.reshape(t, GLA_V_W), o_b.reshape(t, GLA_V_W), o_m.reshape(t, MLA_V_W),
               act, w["gla_g"], w["post_g"], w["w_a"], w["w_b"], w["w_o"], tl["ts"])
    return y.reshape(b, s, D_MODEL)


def kernel(x_prompt, x_sample, pre_norm_g, w_in, gla_wg2_fwd, gla_bg2_fwd, gla_wg2_bwd, gla_bg2_bwd,
           gla_out_norm_g, mla_q_norm_g, mla_w_uq, mla_kv_norm_g, mla_w_ukv, w_branch_a, w_branch_b,
           w_out, post_norm_g):
    depth = pre_norm_g.shape[0]
    y_prompt, y_sample = x_prompt, x_sample
    for l in range(depth):
        w = _prepare_weights(pre_norm_g[l], w_in[l], gla_wg2_fwd[l], gla_bg2_fwd[l], gla_wg2_bwd[l],
                             gla_bg2_bwd[l], gla_out_norm_g[l], mla_q_norm_g[l], mla_w_uq[l],
                             mla_kv_norm_g[l], mla_w_ukv[l], w_branch_a[l], w_branch_b[l],
                             w_out[l], post_norm_g[l])
        y_prompt = _layer(y_prompt, w)
        y_sample = _layer(y_sample, w)
    return (y_prompt, y_sample)
```

```python
import functools
import math

import jax
import jax.numpy as jnp
from jax import lax
from jax.experimental import pallas as pl
from jax.experimental.pallas import tpu as pltpu

D_MODEL = 1024
GLA_HEADS = 4
GLA_DK = 128
GLA_DV = 256
GLA_GATE_RANK = 16
GLA_GATE_NORM = 16.0
MLA_HEADS = 8
MLA_Q_RANK = 256
MLA_KV_RANK = 128
MLA_NOPE = 128
MLA_ROPE = 64
MLA_V = 128
ROPE_THETA = 10000.0
EPS = 1e-6

GLA_QK_W = GLA_HEADS * GLA_DK
GLA_V_W = GLA_HEADS * GLA_DV
MLA_V_W = MLA_HEADS * MLA_V
MLA_QK = MLA_NOPE + MLA_ROPE

V7X_LANES = 128
V7X_VMEM_LIMIT_BYTES = 56 * 1024 * 1024

QKV_W = 2 * GLA_QK_W + GLA_V_W
LAT_CQ = 0
LAT_CKV = MLA_Q_RANK
LAT_KR = LAT_CKV + MLA_KV_RANK
LAT_KROT = LAT_KR + V7X_LANES
LAT_GR = LAT_KROT + V7X_LANES
LAT_W = LAT_GR + V7X_LANES
ACT_W = 4 * D_MODEL

GLA_KERNEL_CHUNK = 128
GLA_EXP_CLAMP = 80.0
LOG2E = 1.4426950408889634
ATTN_QSUB = 256
ATTN_LOOKAHEAD = 4


def _cparams(semantics):
    return pltpu.CompilerParams(dimension_semantics=semantics,
                                vmem_limit_bytes=V7X_VMEM_LIMIT_BYTES)


def _const_spec(shape):
    return pl.BlockSpec(shape, lambda *_: (0,) * len(shape))


def _rms(x, g):
    return x * lax.rsqrt(jnp.mean(x * x, axis=-1, keepdims=True) + EPS) * g


def _sigmoid(x):
    return 1.0 / (1.0 + jnp.exp(-x))


def _bdot(a, b):
    return jnp.dot(a, b, preferred_element_type=jnp.float32)


def _bdot_nt(a, b):
    return lax.dot_general(a, b, (((1,), (1,)), ((), ())), preferred_element_type=jnp.float32)


def _bdot_tn(a, b):
    return lax.dot_general(a, b, (((0,), (0,)), ((), ())), preferred_element_type=jnp.float32)


def _proj_kernel(x_ref, g_ref, wqkv_ref, wlat_ref, wact_ref, qkv_ref, lat_ref, act_ref):
    h = _rms(x_ref[...], g_ref[...]).astype(jnp.bfloat16)
    qkv = _bdot(h, wqkv_ref[...])
    qkv_ref[:, :GLA_QK_W] = (qkv[:, :GLA_QK_W] * (GLA_DK ** -0.5)).astype(jnp.bfloat16)
    qkv_ref[:, GLA_QK_W:] = qkv[:, GLA_QK_W:].astype(jnp.bfloat16)
    lat_ref[...] = _bdot(h, wlat_ref[...])
    z = _bdot(h, wact_ref[:, :2 * D_MODEL])
    act_ref[:, :2 * D_MODEL] = (z * _sigmoid(z)).astype(jnp.bfloat16)
    gate = _bdot(h, wact_ref[:, 2 * D_MODEL:])
    act_ref[:, 2 * D_MODEL:] = _sigmoid(gate).astype(jnp.bfloat16)


def _proj(x2d, pre_g, wqkv, wlat, wact, ts):
    t = x2d.shape[0]
    return pl.pallas_call(
        _proj_kernel,
        grid=(t // ts,),
        in_specs=[pl.BlockSpec((ts, D_MODEL), lambda i: (i, 0)),
                  _const_spec((1, D_MODEL)),
                  _const_spec((D_MODEL, QKV_W)),
                  _const_spec((D_MODEL, LAT_W)),
                  _const_spec((D_MODEL, ACT_W))],
        out_specs=[pl.BlockSpec((ts, QKV_W), lambda i: (i, 0)),
                   pl.BlockSpec((ts, LAT_W), lambda i: (i, 0)),
                   pl.BlockSpec((ts, ACT_W), lambda i: (i, 0))],
        out_shape=[jax.ShapeDtypeStruct((t, QKV_W), jnp.bfloat16),
                   jax.ShapeDtypeStruct((t, LAT_W), jnp.float32),
                   jax.ShapeDtypeStruct((t, ACT_W), jnp.bfloat16)],
        compiler_params=_cparams(("parallel",)),
        name="proj",
    )(x2d, pre_g, wqkv, wlat, wact)


def _mla_up_kernel(lat_ref, cos_ref, sin_ref, qg_ref, kvg_ref, wqn_ref, wqr_ref, wqrot_ref,
                   wkn_ref, wv_ref, qt_ref, kf_ref, vt_ref):
    lat = lat_ref[...]
    cos = cos_ref[...]
    sin = sin_ref[...]
    c_q = _rms(lat[:, LAT_CQ:LAT_CQ + MLA_Q_RANK], qg_ref[...]).astype(jnp.bfloat16)
    c_kv = _rms(lat[:, LAT_CKV:LAT_CKV + MLA_KV_RANK], kvg_ref[...]).astype(jnp.bfloat16)

    q_scale = (MLA_QK ** -0.5) * LOG2E
    qn_t = (_bdot(c_q, wqn_ref[...]) * q_scale).T
    cos_h = jnp.concatenate([cos, cos], axis=1)
    sin_h = jnp.concatenate([sin, sin], axis=1)
    cos_q = jnp.concatenate([cos_h] * (MLA_HEADS // 2), axis=1)
    sin_q = jnp.concatenate([sin_h] * (MLA_HEADS // 2), axis=1)
    q_pe = _bdot(c_q, wqr_ref[...]) * cos_q + _bdot(c_q, wqrot_ref[...]) * sin_q
    qpe_t = (q_pe * q_scale).T

    k_pe = (lat[:, LAT_KR:LAT_KR + MLA_ROPE] * cos
            + lat[:, LAT_KROT:LAT_KROT + MLA_ROPE] * sin).astype(jnp.bfloat16)
    kn = _bdot(c_kv, wkn_ref[...]).astype(jnp.bfloat16)
    v_t = _bdot(c_kv, wv_ref[...]).T

    for h in range(MLA_HEADS):
        qt_ref[h, :MLA_NOPE, :] = qn_t[h * MLA_NOPE:(h + 1) * MLA_NOPE, :].astype(jnp.bfloat16)
        qt_ref[h, MLA_NOPE:, :] = qpe_t[h * MLA_ROPE:(h + 1) * MLA_ROPE, :].astype(jnp.bfloat16)
        kf_ref[h, :, :MLA_NOPE] = kn[:, h * MLA_NOPE:(h + 1) * MLA_NOPE]
        kf_ref[h, :, MLA_NOPE:] = k_pe
        vt_ref[h, :, :] = v_t[h * MLA_V:(h + 1) * MLA_V, :].astype(jnp.bfloat16)


def _mla_up(lat3, cos, sin, qg, kvg, wqn, wqr, wqrot, wkn, wv, ts):
    b, s, _ = lat3.shape
    hh = MLA_HEADS
    return pl.pallas_call(
        _mla_up_kernel,
        grid=(b, s // ts),
        in_specs=[pl.BlockSpec((None, ts, LAT_W), lambda i, j: (i, j, 0)),
                  pl.BlockSpec((ts, MLA_ROPE), lambda i, j: (j, 0)),
                  pl.BlockSpec((ts, MLA_ROPE), lambda i, j: (j, 0)),
                  _const_spec((1, MLA_Q_RANK)),
                  _const_spec((1, MLA_KV_RANK)),
                  _const_spec((MLA_Q_RANK, hh * MLA_NOPE)),
                  _const_spec((MLA_Q_RANK, hh * MLA_ROPE)),
                  _const_spec((MLA_Q_RANK, hh * MLA_ROPE)),
                  _const_spec((MLA_KV_RANK, hh * MLA_NOPE)),
                  _const_spec((MLA_KV_RANK, hh * MLA_V))],
        out_specs=[pl.BlockSpec((None, hh, MLA_QK, ts), lambda i, j: (i, 0, 0, j)),
                   pl.BlockSpec((None, hh, ts, MLA_QK), lambda i, j: (i, 0, j, 0)),
                   pl.BlockSpec((None, hh, None, MLA_V, ts), lambda i, j: (i, 0, j, 0, 0))],
        out_shape=[jax.ShapeDtypeStruct((b, hh, MLA_QK, s), jnp.bfloat16),
                   jax.ShapeDtypeStruct((b, hh, s, MLA_QK), jnp.bfloat16),
                   jax.ShapeDtypeStruct((b, hh, s // ts, MLA_V, ts), jnp.bfloat16)],
        compiler_params=_cparams(("parallel", "parallel")),
        name="mla_up",
    )(lat3, cos, sin, qg, kvg, wqn, wqr, wqrot, wkn, wv)


def _attn_kernel(qt_ref, k_ref, vt_ref, o_ref, m_ref, l_ref, acc_ref, *, unroll):
    n_chunks, _, tk = vt_ref.shape
    n_sub = qt_ref.shape[1] // ATTN_QSUB
    m_ref[...] = jnp.full_like(m_ref, -jnp.inf)
    l_ref[...] = jnp.zeros_like(l_ref)
    acc_ref[...] = jnp.zeros_like(acc_ref)

    def scores(c, j):
        rows = pl.ds(pl.multiple_of(c * tk, tk), tk)
        return _bdot(k_ref[rows, :], qt_ref[:, j * ATTN_QSUB:(j + 1) * ATTN_QSUB])

    def group(g, carry):
        chains = [(g * unroll + u, j) for u in range(unroll) for j in range(n_sub)]
        pending = [scores(*ch) for ch in chains[:ATTN_LOOKAHEAD]]
        for idx, (c, j) in enumerate(chains):
            cols = slice(j * ATTN_QSUB, (j + 1) * ATTN_QSUB)
            s = pending.pop(0)
            if idx + ATTN_LOOKAHEAD < len(chains):
                pending.append(scores(*chains[idx + ATTN_LOOKAHEAD]))
            m_old = m_ref[:, cols]
            m_new = jnp.maximum(m_old, jnp.max(s, axis=0, keepdims=True))
            alpha = jnp.exp2(m_old - m_new)
            p = jnp.exp2(s - m_new)
            l_ref[:, cols] = alpha * l_ref[:, cols] + jnp.sum(p, axis=0, keepdims=True)
            acc_ref[:, cols] = alpha * acc_ref[:, cols] + _bdot(vt_ref[c], p.astype(jnp.bfloat16))
            m_ref[:, cols] = m_new
        return carry

    n_groups = n_chunks // unroll
    if n_groups == 1:
        group(0, 0)
    else:
        lax.fori_loop(0, n_groups, group, 0)
    o_ref[...] = (acc_ref[...] / l_ref[...]).T.astype(o_ref.dtype)


def _attn(qt, kf, vt, tq, unroll):
    b, hh, _, s = qt.shape
    n_chunks, _, tk = vt.shape[2:]
    return pl.pallas_call(
        functools.partial(_attn_kernel, unroll=unroll),
        grid=(b, hh, s // tq),
        in_specs=[pl.BlockSpec((None, None, MLA_QK, tq), lambda i, h, q: (i, h, 0, q)),
                  pl.BlockSpec((None, None, s, MLA_QK), lambda i, h, q: (i, h, 0, 0)),
                  pl.BlockSpec((None, None, n_chunks, MLA_V, tk), lambda i, h, q: (i, h, 0, 0, 0))],
        out_specs=pl.BlockSpec((None, tq, MLA_V), lambda i, h, q: (i, q, h)),
        out_shape=jax.ShapeDtypeStruct((b, s, MLA_V_W), jnp.bfloat16),
        scratch_shapes=[pltpu.VMEM((1, tq), jnp.float32),
                        pltpu.VMEM((1, tq), jnp.float32),
                        pltpu.VMEM((MLA_V, tq), jnp.float32)],
        compiler_params=_cparams(("parallel", "parallel", "parallel")),
        name="attn",
    )(qt, kf, vt)


def _split3(x):
    hi = x.astype(jnp.bfloat16)
    r1 = x - hi.astype(jnp.float32)
    mid = r1.astype(jnp.bfloat16)
    lo = (r1 - mid.astype(jnp.float32)).astype(jnp.bfloat16)
    return hi, mid, lo


def _gla_direction(qkv_ref, lat_ref, wg_ref, bg_ref, o_ref, state_ref, reverse):
    c = GLA_KERNEL_CHUNK
    tb = qkv_ref.shape[0]
    n_chunks = tb // c
    row = lax.broadcasted_iota(jnp.int32, (c, c), 0)
    col = lax.broadcasted_iota(jnp.int32, (c, c), 1)
    if reverse:
        tri = (col >= row).astype(jnp.bfloat16)
        keep = col > row
        last = 0
    else:
        tri = (col <= row).astype(jnp.bfloat16)
        keep = col <= row
        last = c - 1
    mid = c // 2

    def chunk_body(ci, carry):
        cidx = (n_chunks - 1 - ci) if reverse else ci
        r0 = pl.multiple_of(cidx * c, c)
        rows = pl.ds(r0, c)
        x = _bdot(lat_ref[rows, :].astype(jnp.bfloat16), wg_ref[...]) + bg_ref[...]
        g = (jnp.minimum(x, 0.0) - jnp.log1p(jnp.exp(-jnp.abs(x)))) * (1.0 / GLA_GATE_NORM)
        g_hi, g_mid, g_lo = _split3(g)
        b = _bdot(tri, g_hi) + _bdot(tri, g_mid) + _bdot(tri, g_lo)
        b_mid = b[mid:mid + 1, :]
        b_last = b[last:last + 1, :]
        d = b - b_mid
        e_q = jnp.exp(jnp.minimum(d, GLA_EXP_CLAMP))
        e_k = jnp.exp(jnp.minimum(-d, GLA_EXP_CLAMP))
        q = qkv_ref[rows, 0:GLA_QK_W].astype(jnp.float32)
        k = qkv_ref[rows, GLA_QK_W:2 * GLA_QK_W].astype(jnp.float32)
        q_in = (q * e_q).astype(jnp.bfloat16)
        k_in = (k * e_k).astype(jnp.bfloat16)
        q_st = (q * jnp.exp(b)).astype(jnp.bfloat16)
        k_st = (k * jnp.exp(b_last - b)).astype(jnp.bfloat16)
        e_last = jnp.exp(b_last)
        for h in range(GLA_HEADS):
            ks = slice(h * GLA_DK, (h + 1) * GLA_DK)
            v = qkv_ref[rows, 2 * GLA_QK_W + h * GLA_DV:2 * GLA_QK_W + (h + 1) * GLA_DV]
            a = _bdot_nt(q_in[:, ks], k_in[:, ks])
            a = jnp.where(keep, a, 0.0).astype(jnp.bfloat16)
            st = state_ref[h]
            o = _bdot(a, v) + _bdot(q_st[:, ks], st.astype(jnp.bfloat16))
            o_ref[rows, h * GLA_DV:(h + 1) * GLA_DV] = o.astype(o_ref.dtype)
            upd = _bdot_tn(k_st[:, ks], v)
            decay_col = jnp.broadcast_to(e_last[:, ks], (GLA_DK, GLA_DK)).T
            decay = jnp.concatenate([decay_col] * (GLA_DV // GLA_DK), axis=1)
            state_ref[h] = st * decay + upd
        return carry

    lax.fori_loop(0, n_chunks, chunk_body, 0)


def _gla_kernel(qkv_f_ref, lat_f_ref, qkv_b_ref, lat_b_ref, wgf_ref, bgf_ref, wgb_ref, bgb_ref,
                of_ref, ob_ref, sf_ref, sb_ref):
    @pl.when(pl.program_id(1) == 0)
    def _():
        sf_ref[...] = jnp.zeros_like(sf_ref)
        sb_ref[...] = jnp.zeros_like(sb_ref)

    _gla_direction(qkv_f_ref, lat_f_ref, wgf_ref, bgf_ref, of_ref, sf_ref, reverse=False)
    _gla_direction(qkv_b_ref, lat_b_ref, wgb_ref, bgb_ref, ob_ref, sb_ref, reverse=True)


def _gla(qkv3, lat3, wgf, bgf, wgb, bgb, tb):
    b, s, _ = qkv3.shape
    nb = s // tb
    gr_blk = LAT_GR // V7X_LANES
    fwd = lambda i, j: (i, j, 0)
    bwd = lambda i, j: (i, nb - 1 - j, 0)
    return pl.pallas_call(
        _gla_kernel,
        grid=(b, nb),
        in_specs=[pl.BlockSpec((None, tb, QKV_W), fwd),
                  pl.BlockSpec((None, tb, V7X_LANES), lambda i, j: (i, j, gr_blk)),
                  pl.BlockSpec((None, tb, QKV_W), bwd),
                  pl.BlockSpec((None, tb, V7X_LANES), lambda i, j: (i, nb - 1 - j, gr_blk)),
                  _const_spec((V7X_LANES, GLA_QK_W)),
                  _const_spec((1, GLA_QK_W)),
                  _const_spec((V7X_LANES, GLA_QK_W)),
                  _const_spec((1, GLA_QK_W))],
        out_specs=[pl.BlockSpec((None, tb, GLA_V_W), fwd),
                   pl.BlockSpec((None, tb, GLA_V_W), bwd)],
        out_shape=[jax.ShapeDtypeStruct((b, s, GLA_V_W), jnp.bfloat16),
                   jax.ShapeDtypeStruct((b, s, GLA_V_W), jnp.bfloat16)],
        scratch_shapes=[pltpu.VMEM((GLA_HEADS, GLA_DK, GLA_DV), jnp.float32),
                        pltpu.VMEM((GLA_HEADS, GLA_DK, GLA_DV), jnp.float32)],
        compiler_params=_cparams(("parallel", "arbitrary")),
        name="gla",
    )(qkv3, lat3, qkv3, lat3, wgf, bgf, wgb, bgb)


def _final_kernel(x_ref, of_ref, ob_ref, om_ref, act_ref, gg_ref, pg_ref, wa_ref, wb_ref, wo_ref,
                  y_ref):
    o_sum = of_ref[...].astype(jnp.float32) + ob_ref[...].astype(jnp.float32)
    gg = gg_ref[...]
    o_a = jnp.concatenate(
        [_rms(o_sum[:, h * GLA_DV:(h + 1) * GLA_DV], gg) for h in range(GLA_HEADS)], axis=1)
    y_a = (o_a * act_ref[:, 0:D_MODEL].astype(jnp.float32)).astype(jnp.bfloat16)
    y_b = om_ref[...] * act_ref[:, D_MODEL:2 * D_MODEL]
    merged = (act_ref[:, 2 * D_MODEL:3 * D_MODEL].astype(jnp.float32) * _bdot(y_a, wa_ref[...])
              + act_ref[:, 3 * D_MODEL:].astype(jnp.float32) * _bdot(y_b, wb_ref[...]))
    out = _bdot(merged.astype(jnp.bfloat16), wo_ref[...])
    y_ref[...] = x_ref[...] + _rms(out, pg_ref[...])


def _final(x2d, o_f, o_b, o_m, act, gg, pg, wa, wb, wo, ts):
    t = x2d.shape[0]
    tok = lambda w: pl.BlockSpec((ts, w), lambda i: (i, 0))
    return pl.pallas_call(
        _final_kernel,
        grid=(t // ts,),
        in_specs=[tok(D_MODEL), tok(GLA_V_W), tok(GLA_V_W), tok(MLA_V_W), tok(ACT_W),
                  _const_spec((1, GLA_DV)), _const_spec((1, D_MODEL)),
                  _const_spec((GLA_V_W, D_MODEL)), _const_spec((MLA_V_W, D_MODEL)),
                  _const_spec((D_MODEL, D_MODEL))],
        out_specs=tok(D_MODEL),
        out_shape=jax.ShapeDtypeStruct((t, D_MODEL), jnp.float32),
        compiler_params=_cparams(("parallel",)),
        name="final",
    )(x2d, o_f, o_b, o_m, act, gg, pg, wa, wb, wo)


def _rotate_half_cols(w):
    half = MLA_ROPE // 2
    w3 = w.reshape(w.shape[0], -1, MLA_ROPE)
    return jnp.concatenate([-w3[..., half:], w3[..., :half]], axis=-1).reshape(w.shape)


def _pad_cols(w, width):
    return jnp.pad(w, ((0, 0), (0, width - w.shape[1])))


def _prepare_weights(pre_g, w_in, wg2_f, bg2_f, wg2_b, bg2_b, gla_g, q_g, w_uq, kv_g, w_ukv,
                     w_a, w_b, w_o, post_g):
    bf = jnp.bfloat16
    sizes = (GLA_QK_W, GLA_QK_W, GLA_V_W, GLA_GATE_RANK, GLA_GATE_RANK, GLA_V_W,
             MLA_Q_RANK, MLA_KV_RANK, MLA_ROPE, MLA_V_W, D_MODEL, D_MODEL)
    offs = [0]
    for sz in sizes:
        offs.append(offs[-1] + sz)
    (w_q, w_k, w_v, w_grf, w_grb, w_za, w_cq, w_ckv, w_kr, w_zb, w_ga, w_gb) = [
        w_in[:, offs[i]:offs[i + 1]] for i in range(len(sizes))]
    wqkv = jnp.concatenate([w_q, w_k, w_v], axis=1).astype(bf)
    wlat = jnp.concatenate([
        w_cq, w_ckv, _pad_cols(w_kr, V7X_LANES), _pad_cols(_rotate_half_cols(w_kr), V7X_LANES),
        _pad_cols(jnp.concatenate([w_grf, w_grb], axis=1), V7X_LANES)], axis=1).astype(bf)
    wact = jnp.concatenate([w_za, w_zb, w_ga, w_gb], axis=1).astype(bf)

    zeros_gate = jnp.zeros((V7X_LANES, GLA_QK_W), jnp.float32)
    wgf = zeros_gate.at[0:GLA_GATE_RANK].set(wg2_f).astype(bf)
    wgb = zeros_gate.at[GLA_GATE_RANK:2 * GLA_GATE_RANK].set(wg2_b).astype(bf)

    uq = w_uq.reshape(MLA_Q_RANK, MLA_HEADS, MLA_QK)
    wqn = uq[:, :, :MLA_NOPE].reshape(MLA_Q_RANK, MLA_HEADS * MLA_NOPE)
    wqr = uq[:, :, MLA_NOPE:].reshape(MLA_Q_RANK, MLA_HEADS * MLA_ROPE)
    ukv = w_ukv.reshape(MLA_KV_RANK, MLA_HEADS, MLA_NOPE + MLA_V)
    wkn = ukv[:, :, :MLA_NOPE].reshape(MLA_KV_RANK, MLA_HEADS * MLA_NOPE)
    wv = ukv[:, :, MLA_NOPE:].reshape(MLA_KV_RANK, MLA_HEADS * MLA_V)
    return dict(
        pre_g=pre_g[None, :], wqkv=wqkv, wlat=wlat, wact=wact,
        wgf=wgf, bgf=bg2_f[None, :], wgb=wgb, bgb=bg2_b[None, :],
        gla_g=gla_g[None, :], q_g=q_g[None, :], kv_g=kv_g[None, :],
        wqn=wqn.astype(bf), wqr=wqr.astype(bf), wqrot=_rotate_half_cols(wqr).astype(bf),
        wkn=wkn.astype(bf), wv=wv.astype(bf),
        w_a=w_a.astype(bf), w_b=w_b.astype(bf), w_o=w_o.astype(bf), post_g=post_g[None, :])


def _rope_tables(s):
    half = MLA_ROPE // 2
    freqs = ROPE_THETA ** (-jnp.arange(half, dtype=jnp.float32) / half)
    ang = jnp.arange(s).astype(jnp.float32)[:, None] * freqs[None, :]
    cos = jnp.cos(ang)
    sin = jnp.sin(ang)
    return jnp.concatenate([cos, cos], axis=1), jnp.concatenate([sin, sin], axis=1)


def _tiles(b, s):
    t = b * s
    tk = min(512, s)
    return dict(ts=min(512, t), tk=tk, tb=min(512, s), tq=min(1024, s),
                attn_unroll=min(8, s // tk))


def _layer(x, w):
    b, s, _ = x.shape
    t = b * s
    tl = _tiles(b, s)
    x2d = x.reshape(t, D_MODEL)
    qkv, lat, act = _proj(x2d, w["pre_g"], w["wqkv"], w["wlat"], w["wact"], tl["ts"])
    lat3 = lat.reshape(b, s, LAT_W)
    cos, sin = _rope_tables(s)
    qt, kf, vt = _mla_up(lat3, cos, sin, w["q_g"], w["kv_g"], w["wqn"], w["wqr"], w["wqrot"],
                         w["wkn"], w["wv"], tl["tk"])
    o_m = _attn(qt, kf, vt, tl["tq"], tl["attn_unroll"])
    o_f, o_b = _gla(qkv.reshape(b, s, QKV_W), lat3, w["wgf"], w["bgf"], w["wgb"], w["bgb"],
                    tl["tb"])
    y = _final(x2d, o_f.reshape(t, GLA_V_W), o_b.reshape(t, GLA_V_W), o_m.reshape(t, MLA_V_W),
               act, w["gla_g"], w["post_g"], w["w_a"], w["w_b"], w["w_o"], tl["ts"])
    return y.reshape(b, s, D_MODEL)


def kernel(x_prompt, x_sample, pre_norm_g, w_in, gla_wg2_fwd, gla_bg2_fwd, gla_wg2_bwd, gla_bg2_bwd,
           gla_out_norm_g, mla_q_norm_g, mla_w_uq, mla_kv_norm_g, mla_w_ukv, w_branch_a, w_branch_b,
           w_out, post_norm_g):
    depth = pre_norm_g.shape[0]
    y_prompt, y_sample = x_prompt, x_sample
    for l in range(depth):
        w = _prepare_weights(pre_norm_g[l], w_in[l], gla_wg2_fwd[l], gla_bg2_fwd[l], gla_wg2_bwd[l],
                             gla_bg2_bwd[l], gla_out_norm_g[l], mla_q_norm_g[l], mla_w_uq[l],
                             mla_kv_norm_g[l], mla_w_ukv[l], w_branch_a[l], w_branch_b[l],
                             w_out[l], post_norm_g[l])
        y_prompt = _layer(y_prompt, w)
        y_sample = _layer(y_sample, w)
    return (y_prompt, y_sample)
```

```python
import functools
import math

import jax
import jax.numpy as jnp
from jax import lax
from jax.experimental import pallas as pl
from jax.experimental.pallas import tpu as pltpu

D_MODEL = 1024
GLA_HEADS = 4
GLA_DK = 128
GLA_DV = 256
GLA_GATE_RANK = 16
GLA_GATE_NORM = 16.0
MLA_HEADS = 8
MLA_Q_RANK = 256
MLA_KV_RANK = 128
MLA_NOPE = 128
MLA_ROPE = 64
MLA_V = 128
ROPE_THETA = 10000.0
EPS = 1e-6

GLA_QK_W = GLA_HEADS * GLA_DK
GLA_V_W = GLA_HEADS * GLA_DV
MLA_V_W = MLA_HEADS * MLA_V
MLA_QK = MLA_NOPE + MLA_ROPE

V7X_LANES = 128
V7X_VMEM_LIMIT_BYTES = 56 * 1024 * 1024

QKV_W = 2 * GLA_QK_W + GLA_V_W
LAT_CQ = 0
LAT_CKV = MLA_Q_RANK
LAT_KR = LAT_CKV + MLA_KV_RANK
LAT_KROT = LAT_KR + V7X_LANES
LAT_GR = LAT_KROT + V7X_LANES
LAT_W = LAT_GR + V7X_LANES
ACT_W = 4 * D_MODEL
GATES_W = 2 * GLA_QK_W

GLA_KERNEL_CHUNK = 128
GLA_EXP2_CLAMP = 115.0
LOG2E = 1.4426950408889634
ATTN_QSUB = 256
ATTN_LOOKAHEAD = 4


def _cparams(semantics):
    return pltpu.CompilerParams(dimension_semantics=semantics,
                                vmem_limit_bytes=V7X_VMEM_LIMIT_BYTES)


def _const_spec(shape):
    return pl.BlockSpec(shape, lambda *_: (0,) * len(shape))


def _rms(x, g):
    return x * lax.rsqrt(jnp.mean(x * x, axis=-1, keepdims=True) + EPS) * g


def _sigmoid(x):
    return 1.0 / (1.0 + jnp.exp(-x))


def _bdot(a, b):
    return jnp.dot(a, b, preferred_element_type=jnp.float32)


def _bdot_nt(a, b):
    return lax.dot_general(a, b, (((1,), (1,)), ((), ())), preferred_element_type=jnp.float32)


def _bdot_tn(a, b):
    return lax.dot_general(a, b, (((0,), (0,)), ((), ())), preferred_element_type=jnp.float32)


def _proj_kernel(x_ref, g_ref, wqkv_ref, wlat_ref, wact_ref, wg_ref, bg_ref,
                 qkv_ref, lat_ref, act_ref, gates_ref):
    half = x_ref.shape[0] // 2
    for r in (slice(0, half), slice(half, 2 * half)):
        h = _rms(x_ref[r, :], g_ref[...]).astype(jnp.bfloat16)
        lat = _bdot(h, wlat_ref[...])
        xg = _bdot(lat[:, LAT_GR:].astype(jnp.bfloat16), wg_ref[...]) + bg_ref[...]
        qkv = _bdot(h, wqkv_ref[...])
        lat_ref[r, :] = lat
        gates_ref[r, :] = ((jnp.minimum(xg, 0.0) - jnp.log(1.0 + jnp.exp(-jnp.abs(xg))))
                           * (LOG2E / GLA_GATE_NORM))
        z = _bdot(h, wact_ref[:, :2 * D_MODEL])
        qkv_ref[r, :GLA_QK_W] = (qkv[:, :GLA_QK_W] * (GLA_DK ** -0.5)).astype(jnp.bfloat16)
        qkv_ref[r, GLA_QK_W:] = qkv[:, GLA_QK_W:].astype(jnp.bfloat16)
        gate = _bdot(h, wact_ref[:, 2 * D_MODEL:])
        act_ref[r, :2 * D_MODEL] = (z * _sigmoid(z)).astype(jnp.bfloat16)
        act_ref[r, 2 * D_MODEL:] = _sigmoid(gate).astype(jnp.bfloat16)


def _proj(x2d, pre_g, wqkv, wlat, wact, wg, bg, ts):
    t = x2d.shape[0]
    return pl.pallas_call(
        _proj_kernel,
        grid=(t // ts,),
        in_specs=[pl.BlockSpec((ts, D_MODEL), lambda i: (i, 0)),
                  _const_spec((1, D_MODEL)),
                  _const_spec((D_MODEL, QKV_W)),
                  _const_spec((D_MODEL, LAT_W)),
                  _const_spec((D_MODEL, ACT_W)),
                  _const_spec((V7X_LANES, GATES_W)),
                  _const_spec((1, GATES_W))],
        out_specs=[pl.BlockSpec((ts, QKV_W), lambda i: (i, 0)),
                   pl.BlockSpec((ts, LAT_W), lambda i: (i, 0)),
                   pl.BlockSpec((ts, ACT_W), lambda i: (i, 0)),
                   pl.BlockSpec((ts, GATES_W), lambda i: (i, 0))],
        out_shape=[jax.ShapeDtypeStruct((t, QKV_W), jnp.bfloat16),
                   jax.ShapeDtypeStruct((t, LAT_W), jnp.float32),
                   jax.ShapeDtypeStruct((t, ACT_W), jnp.bfloat16),
                   jax.ShapeDtypeStruct((t, GATES_W), jnp.float32)],
        compiler_params=_cparams(("parallel",)),
        name="proj",
    )(x2d, pre_g, wqkv, wlat, wact, wg, bg)


def _mla_up_kernel(lat_ref, cos_ref, sin_ref, qg_ref, kvg_ref, wqn_ref, wqr_ref, wqrot_ref,
                   wkn_ref, wv_ref, qt_ref, kf_ref, vt_ref):
    lat = lat_ref[...]
    cos = cos_ref[...]
    sin = sin_ref[...]
    c_q = _rms(lat[:, LAT_CQ:LAT_CQ + MLA_Q_RANK], qg_ref[...]).astype(jnp.bfloat16)
    c_kv = _rms(lat[:, LAT_CKV:LAT_CKV + MLA_KV_RANK], kvg_ref[...]).astype(jnp.bfloat16)

    q_scale = (MLA_QK ** -0.5) * LOG2E
    qn_t = (_bdot(c_q, wqn_ref[...]) * q_scale).T
    cos_h = jnp.concatenate([cos, cos], axis=1)
    sin_h = jnp.concatenate([sin, sin], axis=1)
    cos_q = jnp.concatenate([cos_h] * (MLA_HEADS // 2), axis=1)
    sin_q = jnp.concatenate([sin_h] * (MLA_HEADS // 2), axis=1)
    q_pe = _bdot(c_q, wqr_ref[...]) * cos_q + _bdot(c_q, wqrot_ref[...]) * sin_q
    qpe_t = (q_pe * q_scale).T

    k_pe = (lat[:, LAT_KR:LAT_KR + MLA_ROPE] * cos
            + lat[:, LAT_KROT:LAT_KROT + MLA_ROPE] * sin).astype(jnp.bfloat16)
    kn = _bdot(c_kv, wkn_ref[...]).astype(jnp.bfloat16)
    v_t = _bdot(c_kv, wv_ref[...]).T

    for h in range(MLA_HEADS):
        qt_ref[h, :MLA_NOPE, :] = qn_t[h * MLA_NOPE:(h + 1) * MLA_NOPE, :].astype(jnp.bfloat16)
        qt_ref[h, MLA_NOPE:, :] = qpe_t[h * MLA_ROPE:(h + 1) * MLA_ROPE, :].astype(jnp.bfloat16)
        kf_ref[h, :, :MLA_NOPE] = kn[:, h * MLA_NOPE:(h + 1) * MLA_NOPE]
        kf_ref[h, :, MLA_NOPE:] = k_pe
        vt_ref[h, :, :] = v_t[h * MLA_V:(h + 1) * MLA_V, :].astype(jnp.bfloat16)


def _mla_up(lat3, cos, sin, qg, kvg, wqn, wqr, wqrot, wkn, wv, ts):
    b, s, _ = lat3.shape
    hh = MLA_HEADS
    return pl.pallas_call(
        _mla_up_kernel,
        grid=(b, s // ts),
        in_specs=[pl.BlockSpec((None, ts, LAT_W), lambda i, j: (i, j, 0)),
                  pl.BlockSpec((ts, MLA_ROPE), lambda i, j: (j, 0)),
                  pl.BlockSpec((ts, MLA_ROPE), lambda i, j: (j, 0)),
                  _const_spec((1, MLA_Q_RANK)),
                  _const_spec((1, MLA_KV_RANK)),
                  _const_spec((MLA_Q_RANK, hh * MLA_NOPE)),
                  _const_spec((MLA_Q_RANK, hh * MLA_ROPE)),
                  _const_spec((MLA_Q_RANK, hh * MLA_ROPE)),
                  _const_spec((MLA_KV_RANK, hh * MLA_NOPE)),
                  _const_spec((MLA_KV_RANK, hh * MLA_V))],
        out_specs=[pl.BlockSpec((None, hh, MLA_QK, ts), lambda i, j: (i, 0, 0, j)),
                   pl.BlockSpec((None, hh, ts, MLA_QK), lambda i, j: (i, 0, j, 0)),
                   pl.BlockSpec((None, hh, None, MLA_V, ts), lambda i, j: (i, 0, j, 0, 0))],
        out_shape=[jax.ShapeDtypeStruct((b, hh, MLA_QK, s), jnp.bfloat16),
                   jax.ShapeDtypeStruct((b, hh, s, MLA_QK), jnp.bfloat16),
                   jax.ShapeDtypeStruct((b, hh, s // ts, MLA_V, ts), jnp.bfloat16)],
        compiler_params=_cparams(("parallel", "parallel")),
        name="mla_up",
    )(lat3, cos, sin, qg, kvg, wqn, wqr, wqrot, wkn, wv)


def _attn_kernel(qt_ref, k_ref, vt_ref, o_ref, m_ref, l_ref, acc_ref, *, unroll):
    n_chunks, _, tk = vt_ref.shape
    n_sub = qt_ref.shape[1] // ATTN_QSUB
    m_ref[...] = jnp.full_like(m_ref, -jnp.inf)
    l_ref[...] = jnp.zeros_like(l_ref)
    acc_ref[...] = jnp.zeros_like(acc_ref)

    def scores(c, j):
        rows = pl.ds(pl.multiple_of(c * tk, tk), tk)
        return _bdot(k_ref[rows, :], qt_ref[:, j * ATTN_QSUB:(j + 1) * ATTN_QSUB])

    def group(g, carry):
        chains = [(g * unroll + u, j) for u in range(unroll) for j in range(n_sub)]
        pending = [scores(*ch) for ch in chains[:ATTN_LOOKAHEAD]]
        for idx, (c, j) in enumerate(chains):
            cols = slice(j * ATTN_QSUB, (j + 1) * ATTN_QSUB)
            s = pending.pop(0)
            if idx + ATTN_LOOKAHEAD < len(chains):
                pending.append(scores(*chains[idx + ATTN_LOOKAHEAD]))
            m_old = m_ref[:, cols]
            m_new = jnp.maximum(m_old, jnp.max(s, axis=0, keepdims=True))
            alpha = jnp.exp2(m_old - m_new)
            p = jnp.exp2(s - m_new)
            l_ref[:, cols] = alpha * l_ref[:, cols] + jnp.sum(p, axis=0, keepdims=True)
            acc_ref[:, cols] = alpha * acc_ref[:, cols] + _bdot(vt_ref[c], p.astype(jnp.bfloat16))
            m_ref[:, cols] = m_new
        return carry

    n_groups = n_chunks // unroll
    if n_groups == 1:
        group(0, 0)
    else:
        lax.fori_loop(0, n_groups, group, 0)
    o_ref[...] = (acc_ref[...] / l_ref[...]).T.astype(o_ref.dtype)


def _attn(qt, kf, vt, tq, unroll):
    b, hh, _, s = qt.shape
    n_chunks, _, tk = vt.shape[2:]
    return pl.pallas_call(
        functools.partial(_attn_kernel, unroll=unroll),
        grid=(b, hh, s // tq),
        in_specs=[pl.BlockSpec((None, None, MLA_QK, tq), lambda i, h, q: (i, h, 0, q)),
                  pl.BlockSpec((None, None, s, MLA_QK), lambda i, h, q: (i, h, 0, 0)),
                  pl.BlockSpec((None, None, n_chunks, MLA_V, tk), lambda i, h, q: (i, h, 0, 0, 0))],
        out_specs=pl.BlockSpec((None, tq, MLA_V), lambda i, h, q: (i, q, h)),
        out_shape=jax.ShapeDtypeStruct((b, s, MLA_V_W), jnp.bfloat16),
        scratch_shapes=[pltpu.VMEM((1, tq), jnp.float32),
                        pltpu.VMEM((1, tq), jnp.float32),
                        pltpu.VMEM((MLA_V, tq), jnp.float32)],
        compiler_params=_cparams(("parallel", "parallel", "parallel")),
        name="attn",
    )(qt, kf, vt)


def _split2(x):
    hi = x.astype(jnp.bfloat16)
    lo = (x - hi.astype(jnp.float32)).astype(jnp.bfloat16)
    return hi, lo


def _gla_chunk_local(qkv_ref, g, rows, reverse):
    c = GLA_KERNEL_CHUNK
    row = lax.broadcasted_iota(jnp.int32, (c, c), 0)
    col = lax.broadcasted_iota(jnp.int32, (c, c), 1)
    if reverse:
        tri = (col >= row).astype(jnp.bfloat16)
        keep = col > row
        last = 0
    else:
        tri = (col <= row).astype(jnp.bfloat16)
        keep = col <= row
        last = c - 1
    mid = c // 2
    g_hi, g_lo = _split2(g)
    b = _bdot(jnp.concatenate([tri, tri], axis=1),
              jnp.concatenate([g_hi, g_lo], axis=0))
    b_mid = b[mid:mid + 1, :]
    b_last = b[last:last + 1, :]
    d = b - b_mid
    e_q = jnp.exp2(jnp.minimum(d, GLA_EXP2_CLAMP))
    e_k = jnp.exp2(jnp.minimum(-d, GLA_EXP2_CLAMP))
    q = qkv_ref[rows, 0:GLA_QK_W].astype(jnp.float32)
    k = qkv_ref[rows, GLA_QK_W:2 * GLA_QK_W].astype(jnp.float32)
    q_in = (q * e_q).astype(jnp.bfloat16)
    k_in = (k * e_k).astype(jnp.bfloat16)
    q_st = (q * jnp.exp2(b)).astype(jnp.bfloat16)
    k_st = (k * jnp.exp2(b_last - b)).astype(jnp.bfloat16)
    e_last = jnp.exp2(b_last)
    zeros = jnp.zeros((c, GLA_DK), jnp.bfloat16)
    keep2 = jnp.concatenate([keep, keep], axis=1)
    scores = []
    for hp in range(GLA_HEADS // 2):
        k0 = k_in[:, (2 * hp) * GLA_DK:(2 * hp + 1) * GLA_DK]
        k1 = k_in[:, (2 * hp + 1) * GLA_DK:(2 * hp + 2) * GLA_DK]
        k_diag = jnp.concatenate([jnp.concatenate([k0, zeros], axis=1),
                                  jnp.concatenate([zeros, k1], axis=1)], axis=0)
        a2 = _bdot_nt(q_in[:, 2 * hp * GLA_DK:(2 * hp + 2) * GLA_DK], k_diag)
        a2 = jnp.where(keep2, a2, 0.0).astype(jnp.bfloat16)
        scores += [a2[:, :c], a2[:, c:]]
    heads = []
    for h in range(GLA_HEADS):
        ks = slice(h * GLA_DK, (h + 1) * GLA_DK)
        v = qkv_ref[rows, 2 * GLA_QK_W + h * GLA_DV:2 * GLA_QK_W + (h + 1) * GLA_DV]
        upd = _bdot_tn(k_st[:, ks], v)
        decay_col = jnp.broadcast_to(e_last[:, ks], (GLA_DK, GLA_DK)).T
        decay = jnp.concatenate([decay_col] * (GLA_DV // GLA_DK), axis=1)
        heads.append((jnp.concatenate([scores[h], q_st[:, ks]], axis=1), v, upd, decay))
    return heads


def _gla_chunk_state(heads, rows, o_ref, state_ref):
    for h, (lhs, v, upd, decay) in enumerate(heads):
        st = state_ref[h]
        o = _bdot(lhs, jnp.concatenate([v, st.astype(jnp.bfloat16)], axis=0))
        o_ref[rows, h * GLA_DV:(h + 1) * GLA_DV] = o.astype(o_ref.dtype)
        state_ref[h] = st * decay + upd


def _gla_kernel(qkv_f_ref, g_f_ref, qkv_b_ref, g_b_ref, of_ref, ob_ref, sf_ref, sb_ref):
    @pl.when(pl.program_id(1) == 0)
    def _():
        sf_ref[...] = jnp.zeros_like(sf_ref)
        sb_ref[...] = jnp.zeros_like(sb_ref)

    c = GLA_KERNEL_CHUNK
    n_chunks = qkv_f_ref.shape[0] // c
    for ci in range(n_chunks):
        rows_f = slice(ci * c, (ci + 1) * c)
        rows_b = slice((n_chunks - 1 - ci) * c, (n_chunks - ci) * c)
        local_f = _gla_chunk_local(qkv_f_ref, g_f_ref[rows_f, :], rows_f, reverse=False)
        local_b = _gla_chunk_local(qkv_b_ref, g_b_ref[rows_b, :], rows_b, reverse=True)
        _gla_chunk_state(local_f, rows_f, of_ref, sf_ref)
        _gla_chunk_state(local_b, rows_b, ob_ref, sb_ref)


def _gla(qkv3, gates3, tb):
    b, s, _ = qkv3.shape
    nb = s // tb
    fwd = lambda i, j: (i, j, 0)
    bwd = lambda i, j: (i, nb - 1 - j, 0)
    return pl.pallas_call(
        _gla_kernel,
        grid=(b, nb),
        in_specs=[pl.BlockSpec((None, tb, QKV_W), fwd),
                  pl.BlockSpec((None, tb, GLA_QK_W), fwd),
                  pl.BlockSpec((None, tb, QKV_W), bwd),
                  pl.BlockSpec((None, tb, GLA_QK_W), lambda i, j: (i, nb - 1 - j, 1))],
        out_specs=[pl.BlockSpec((None, tb, GLA_V_W), fwd),
                   pl.BlockSpec((None, tb, GLA_V_W), bwd)],
        out_shape=[jax.ShapeDtypeStruct((b, s, GLA_V_W), jnp.bfloat16),
                   jax.ShapeDtypeStruct((b, s, GLA_V_W), jnp.bfloat16)],
        scratch_shapes=[pltpu.VMEM((GLA_HEADS, GLA_DK, GLA_DV), jnp.float32),
                        pltpu.VMEM((GLA_HEADS, GLA_DK, GLA_DV), jnp.float32)],
        compiler_params=_cparams(("parallel", "arbitrary")),
        name="gla",
    )(qkv3, gates3, qkv3, gates3)


def _final_kernel(x_ref, of_ref, ob_ref, om_ref, act_ref, gg_ref, pg_ref, wa_ref, wb_ref, wo_ref,
                  y_ref):
    o_sum = of_ref[...].astype(jnp.float32) + ob_ref[...].astype(jnp.float32)
    gg = gg_ref[...]
    o_a = jnp.concatenate(
        [_rms(o_sum[:, h * GLA_DV:(h + 1) * GLA_DV], gg) for h in range(GLA_HEADS)], axis=1)
    y_a = (o_a * act_ref[:, 0:D_MODEL].astype(jnp.float32)).astype(jnp.bfloat16)
    y_b = om_ref[...] * act_ref[:, D_MODEL:2 * D_MODEL]
    merged = (act_ref[:, 2 * D_MODEL:3 * D_MODEL].astype(jnp.float32) * _bdot(y_a, wa_ref[...])
              + act_ref[:, 3 * D_MODEL:].astype(jnp.float32) * _bdot(y_b, wb_ref[...]))
    out = _bdot(merged.astype(jnp.bfloat16), wo_ref[...])
    y_ref[...] = x_ref[...] + _rms(out, pg_ref[...])


def _final(x2d, o_f, o_b, o_m, act, gg, pg, wa, wb, wo, ts):
    t = x2d.shape[0]
    tok = lambda w: pl.BlockSpec((ts, w), lambda i: (i, 0))
    return pl.pallas_call(
        _final_kernel,
        grid=(t // ts,),
        in_specs=[tok(D_MODEL), tok(GLA_V_W), tok(GLA_V_W), tok(MLA_V_W), tok(ACT_W),
                  _const_spec((1, GLA_DV)), _const_spec((1, D_MODEL)),
                  _const_spec((GLA_V_W, D_MODEL)), _const_spec((MLA_V_W, D_MODEL)),
                  _const_spec((D_MODEL, D_MODEL))],
        out_specs=tok(D_MODEL),
        out_shape=jax.ShapeDtypeStruct((t, D_MODEL), jnp.float32),
        compiler_params=_cparams(("parallel",)),
        name="final",
    )(x2d, o_f, o_b, o_m, act, gg, pg, wa, wb, wo)


def _rotate_half_cols(w):
    half = MLA_ROPE // 2
    w3 = w.reshape(w.shape[0], -1, MLA_ROPE)
    return jnp.concatenate([-w3[..., half:], w3[..., :half]], axis=-1).reshape(w.shape)


def _pad_cols(w, width):
    return jnp.pad(w, ((0, 0), (0, width - w.shape[1])))


def _prepare_weights(pre_g, w_in, wg2_f, bg2_f, wg2_b, bg2_b, gla_g, q_g, w_uq, kv_g, w_ukv,
                     w_a, w_b, w_o, post_g):
    bf = jnp.bfloat16
    sizes = (GLA_QK_W, GLA_QK_W, GLA_V_W, GLA_GATE_RANK, GLA_GATE_RANK, GLA_V_W,
             MLA_Q_RANK, MLA_KV_RANK, MLA_ROPE, MLA_V_W, D_MODEL, D_MODEL)
    offs = [0]
    for sz in sizes:
        offs.append(offs[-1] + sz)
    (w_q, w_k, w_v, w_grf, w_grb, w_za, w_cq, w_ckv, w_kr, w_zb, w_ga, w_gb) = [
        w_in[:, offs[i]:offs[i + 1]] for i in range(len(sizes))]
    wqkv = jnp.concatenate([w_q, w_k, w_v], axis=1).astype(bf)
    wlat = jnp.concatenate([
        w_cq, w_ckv, _pad_cols(w_kr, V7X_LANES), _pad_cols(_rotate_half_cols(w_kr), V7X_LANES),
        _pad_cols(jnp.concatenate([w_grf, w_grb], axis=1), V7X_LANES)], axis=1).astype(bf)
    wact = jnp.concatenate([w_za, w_zb, w_ga, w_gb], axis=1).astype(bf)

    wg = jnp.zeros((V7X_LANES, GATES_W), jnp.float32)
    wg = wg.at[0:GLA_GATE_RANK, :GLA_QK_W].set(wg2_f)
    wg = wg.at[GLA_GATE_RANK:2 * GLA_GATE_RANK, GLA_QK_W:].set(wg2_b).astype(bf)
    bg = jnp.concatenate([bg2_f, bg2_b])[None, :]

    uq = w_uq.reshape(MLA_Q_RANK, MLA_HEADS, MLA_QK)
    wqn = uq[:, :, :MLA_NOPE].reshape(MLA_Q_RANK, MLA_HEADS * MLA_NOPE)
    wqr = uq[:, :, MLA_NOPE:].reshape(MLA_Q_RANK, MLA_HEADS * MLA_ROPE)
    ukv = w_ukv.reshape(MLA_KV_RANK, MLA_HEADS, MLA_NOPE + MLA_V)
    wkn = ukv[:, :, :MLA_NOPE].reshape(MLA_KV_RANK, MLA_HEADS * MLA_NOPE)
    wv = ukv[:, :, MLA_NOPE:].reshape(MLA_KV_RANK, MLA_HEADS * MLA_V)
    return dict(
        pre_g=pre_g[None, :], wqkv=wqkv, wlat=wlat, wact=wact,
        wg=wg, bg=bg,
        gla_g=gla_g[None, :], q_g=q_g[None, :], kv_g=kv_g[None, :],
        wqn=wqn.astype(bf), wqr=wqr.astype(bf), wqrot=_rotate_half_cols(wqr).astype(bf),
        wkn=wkn.astype(bf), wv=wv.astype(bf),
        w_a=w_a.astype(bf), w_b=w_b.astype(bf), w_o=w_o.astype(bf), post_g=post_g[None, :])


def _rope_tables(s):
    half = MLA_ROPE // 2
    freqs = ROPE_THETA ** (-jnp.arange(half, dtype=jnp.float32) / half)
    ang = jnp.arange(s).astype(jnp.float32)[:, None] * freqs[None, :]
    cos = jnp.cos(ang)
    sin = jnp.sin(ang)
    return jnp.concatenate([cos, cos], axis=1), jnp.concatenate([sin, sin], axis=1)


def _tiles(b, s):
    t = b * s
    tk = min(512, s)
    return dict(ts=min(512, t), tk=tk, tb=min(512, s), tq=min(1024, s),
                attn_unroll=min(8, s // tk))


def _layer(x, w):
    b, s, _ = x.shape
    t = b * s
    tl = _tiles(b, s)
    x2d = x.reshape(t, D_MODEL)
    qkv, lat, act, gates = _proj(x2d, w["pre_g"], w["wqkv"], w["wlat"], w["wact"], w["wg"], w["bg"],
                                 tl["ts"])
    lat3 = lat.reshape(b, s, LAT_W)
    cos, sin = _rope_tables(s)
    qt, kf, vt = _mla_up(lat3, cos, sin, w["q_g"], w["kv_g"], w["wqn"], w["wqr"], w["wqrot"],
                         w["wkn"], w["wv"], tl["tk"])
    o_m = _attn(qt, kf, vt, tl["tq"], tl["attn_unroll"])
    o_f, o_b = _gla(qkv.reshape(b, s, QKV_W), gates.reshape(b, s, GATES_W), tl["tb"])
    y = _final(x2d, o_f.reshape(t, GLA_V_W), o_b.reshape(t, GLA_V_W), o_m.reshape(t, MLA_V_W),
               act, w["gla_g"], w["post_g"], w["w_a"], w["w_b"], w["w_o"], tl["ts"])
    return y.reshape(b, s, D_MODEL)


def kernel(x_prompt, x_sample, pre_norm_g, w_in, gla_wg2_fwd, gla_bg2_fwd, gla_wg2_bwd, gla_bg2_bwd,
           gla_out_norm_g, mla_q_norm_g, mla_w_uq, mla_kv_norm_g, mla_w_ukv, w_branch_a, w_branch_b,
           w_out, post_norm_g):
    depth = pre_norm_g.shape[0]
    y_prompt, y_sample = x_prompt, x_sample
    for l in range(depth):
        w = _prepare_weights(pre_norm_g[l], w_in[l], gla_wg2_fwd[l], gla_bg2_fwd[l], gla_wg2_bwd[l],
                             gla_bg2_bwd[l], gla_out_norm_g[l], mla_q_norm_g[l], mla_w_uq[l],
                             mla_kv_norm_g[l], mla_w_ukv[l], w_branch_a[l], w_branch_b[l],
                             w_out[l], post_norm_g[l])
        y_prompt = _layer(y_prompt, w)
        y_sample = _layer(y_sample, w)
    return (y_prompt, y_sample)
```

```python
import functools
import math

import jax
import jax.numpy as jnp
from jax import lax
from jax.experimental import pallas as pl
from jax.experimental.pallas import tpu as pltpu

D_MODEL = 1024
GLA_HEADS = 4
GLA_DK = 128
GLA_DV = 256
GLA_GATE_RANK = 16
GLA_GATE_NORM = 16.0
MLA_HEADS = 8
MLA_Q_RANK = 256
MLA_KV_RANK = 128
MLA_NOPE = 128
MLA_ROPE = 64
MLA_V = 128
ROPE_THETA = 10000.0
EPS = 1e-6

GLA_QK_W = GLA_HEADS * GLA_DK
GLA_V_W = GLA_HEADS * GLA_DV
MLA_V_W = MLA_HEADS * MLA_V
MLA_QK = MLA_NOPE + MLA_ROPE

V7X_LANES = 128
V7X_VMEM_LIMIT_BYTES = 56 * 1024 * 1024

QKV_W = 2 * GLA_QK_W + GLA_V_W
LAT_CQ = 0
LAT_CKV = MLA_Q_RANK
LAT_KR = LAT_CKV + MLA_KV_RANK
LAT_KROT = LAT_KR + V7X_LANES
LAT_GR = LAT_KROT + V7X_LANES
LAT_W = LAT_GR + V7X_LANES
ACT_W = 4 * D_MODEL
GATES_W = 2 * GLA_QK_W

GLA_KERNEL_CHUNK = 128
GLA_EXP2_CLAMP = 115.0
LOG2E = 1.4426950408889634
ATTN_QSUB = 256
ATTN_LOOKAHEAD = 3
ATTN_VROWS = MLA_V + 16


def _cparams(semantics):
    return pltpu.CompilerParams(dimension_semantics=semantics,
                                vmem_limit_bytes=V7X_VMEM_LIMIT_BYTES)


def _const_spec(shape):
    return pl.BlockSpec(shape, lambda *_: (0,) * len(shape))


def _rms(x, g):
    return x * lax.rsqrt(jnp.mean(x * x, axis=-1, keepdims=True) + EPS) * g


def _sigmoid(x):
    return 1.0 / (1.0 + jnp.exp(-x))


def _bdot(a, b):
    return jnp.dot(a, b, preferred_element_type=jnp.float32)


def _bdot_nt(a, b):
    return lax.dot_general(a, b, (((1,), (1,)), ((), ())), preferred_element_type=jnp.float32)


def _bdot_tn(a, b):
    return lax.dot_general(a, b, (((0,), (0,)), ((), ())), preferred_element_type=jnp.float32)


def _proj_kernel(x_ref, g_ref, wqkv_ref, wlat_ref, wact_ref, wg_ref, bg_ref,
                 qkv_ref, lat_ref, act_ref, gates_ref):
    half = x_ref.shape[0] // 2
    for r in (slice(0, half), slice(half, 2 * half)):
        h = _rms(x_ref[r, :], g_ref[...]).astype(jnp.bfloat16)
        lat = _bdot(h, wlat_ref[...])
        xg = _bdot(lat[:, LAT_GR:].astype(jnp.bfloat16), wg_ref[...]) + bg_ref[...]
        qkv = _bdot(h, wqkv_ref[...])
        lat_ref[r, :] = lat
        gates_ref[r, :] = ((jnp.minimum(xg, 0.0) - jnp.log(1.0 + jnp.exp(-jnp.abs(xg))))
                           * (LOG2E / GLA_GATE_NORM))
        z = _bdot(h, wact_ref[:, :2 * D_MODEL])
        qkv_ref[r, :GLA_QK_W] = (qkv[:, :GLA_QK_W] * (GLA_DK ** -0.5)).astype(jnp.bfloat16)
        qkv_ref[r, GLA_QK_W:] = qkv[:, GLA_QK_W:].astype(jnp.bfloat16)
        gate = _bdot(h, wact_ref[:, 2 * D_MODEL:])
        act_ref[r, :2 * D_MODEL] = (z * _sigmoid(z)).astype(jnp.bfloat16)
        act_ref[r, 2 * D_MODEL:] = _sigmoid(gate).astype(jnp.bfloat16)


def _proj(x2d, pre_g, wqkv, wlat, wact, wg, bg, ts):
    t = x2d.shape[0]
    return pl.pallas_call(
        _proj_kernel,
        grid=(t // ts,),
        in_specs=[pl.BlockSpec((ts, D_MODEL), lambda i: (i, 0)),
                  _const_spec((1, D_MODEL)),
                  _const_spec((D_MODEL, QKV_W)),
                  _const_spec((D_MODEL, LAT_W)),
                  _const_spec((D_MODEL, ACT_W)),
                  _const_spec((V7X_LANES, GATES_W)),
                  _const_spec((1, GATES_W))],
        out_specs=[pl.BlockSpec((ts, QKV_W), lambda i: (i, 0)),
                   pl.BlockSpec((ts, LAT_W), lambda i: (i, 0)),
                   pl.BlockSpec((ts, ACT_W), lambda i: (i, 0)),
                   pl.BlockSpec((ts, GATES_W), lambda i: (i, 0))],
        out_shape=[jax.ShapeDtypeStruct((t, QKV_W), jnp.bfloat16),
                   jax.ShapeDtypeStruct((t, LAT_W), jnp.float32),
                   jax.ShapeDtypeStruct((t, ACT_W), jnp.bfloat16),
                   jax.ShapeDtypeStruct((t, GATES_W), jnp.float32)],
        compiler_params=_cparams(("parallel",)),
        name="proj",
    )(x2d, pre_g, wqkv, wlat, wact, wg, bg)


def _mla_up_kernel(lat_ref, cos_ref, sin_ref, qg_ref, kvg_ref, wqn_ref, wqr_ref, wqrot_ref,
                   wkn_ref, wv_ref, qt_ref, kf_ref, vt_ref):
    lat = lat_ref[...]
    cos = cos_ref[...]
    sin = sin_ref[...]
    c_q = _rms(lat[:, LAT_CQ:LAT_CQ + MLA_Q_RANK], qg_ref[...]).astype(jnp.bfloat16)
    c_kv = _rms(lat[:, LAT_CKV:LAT_CKV + MLA_KV_RANK], kvg_ref[...]).astype(jnp.bfloat16)

    q_scale = (MLA_QK ** -0.5) * LOG2E
    qn_t = (_bdot(c_q, wqn_ref[...]) * q_scale).T
    cos_h = jnp.concatenate([cos, cos], axis=1)
    sin_h = jnp.concatenate([sin, sin], axis=1)
    cos_q = jnp.concatenate([cos_h] * (MLA_HEADS // 2), axis=1)
    sin_q = jnp.concatenate([sin_h] * (MLA_HEADS // 2), axis=1)
    q_pe = _bdot(c_q, wqr_ref[...]) * cos_q + _bdot(c_q, wqrot_ref[...]) * sin_q
    qpe_t = (q_pe * q_scale).T

    k_pe = (lat[:, LAT_KR:LAT_KR + MLA_ROPE] * cos
            + lat[:, LAT_KROT:LAT_KROT + MLA_ROPE] * sin).astype(jnp.bfloat16)
    kn = _bdot(c_kv, wkn_ref[...]).astype(jnp.bfloat16)
    v_t = _bdot(c_kv, wv_ref[...]).T

    tile_row = lax.broadcasted_iota(jnp.int32, (ATTN_VROWS - MLA_V, vt_ref.shape[2]), 0)
    ones_tile = (tile_row == 0).astype(jnp.bfloat16)
    for h in range(MLA_HEADS):
        qt_ref[h, :MLA_NOPE, :] = qn_t[h * MLA_NOPE:(h + 1) * MLA_NOPE, :].astype(jnp.bfloat16)
        qt_ref[h, MLA_NOPE:, :] = qpe_t[h * MLA_ROPE:(h + 1) * MLA_ROPE, :].astype(jnp.bfloat16)
        kf_ref[h, :, :MLA_NOPE] = kn[:, h * MLA_NOPE:(h + 1) * MLA_NOPE]
        kf_ref[h, :, MLA_NOPE:] = k_pe
        vt_ref[h, :MLA_V, :] = v_t[h * MLA_V:(h + 1) * MLA_V, :].astype(jnp.bfloat16)
        vt_ref[h, MLA_V:, :] = ones_tile


def _mla_up(lat3, cos, sin, qg, kvg, wqn, wqr, wqrot, wkn, wv, ts):
    b, s, _ = lat3.shape
    hh = MLA_HEADS
    return pl.pallas_call(
        _mla_up_kernel,
        grid=(b, s // ts),
        in_specs=[pl.BlockSpec((None, ts, LAT_W), lambda i, j: (i, j, 0)),
                  pl.BlockSpec((ts, MLA_ROPE), lambda i, j: (j, 0)),
                  pl.BlockSpec((ts, MLA_ROPE), lambda i, j: (j, 0)),
                  _const_spec((1, MLA_Q_RANK)),
                  _const_spec((1, MLA_KV_RANK)),
                  _const_spec((MLA_Q_RANK, hh * MLA_NOPE)),
                  _const_spec((MLA_Q_RANK, hh * MLA_ROPE)),
                  _const_spec((MLA_Q_RANK, hh * MLA_ROPE)),
                  _const_spec((MLA_KV_RANK, hh * MLA_NOPE)),
                  _const_spec((MLA_KV_RANK, hh * MLA_V))],
        out_specs=[pl.BlockSpec((None, hh, MLA_QK, ts), lambda i, j: (i, 0, 0, j)),
                   pl.BlockSpec((None, hh, ts, MLA_QK), lambda i, j: (i, 0, j, 0)),
                   pl.BlockSpec((None, hh, None, ATTN_VROWS, ts), lambda i, j: (i, 0, j, 0, 0))],
        out_shape=[jax.ShapeDtypeStruct((b, hh, MLA_QK, s), jnp.bfloat16),
                   jax.ShapeDtypeStruct((b, hh, s, MLA_QK), jnp.bfloat16),
                   jax.ShapeDtypeStruct((b, hh, s // ts, ATTN_VROWS, ts), jnp.bfloat16)],
        compiler_params=_cparams(("parallel", "parallel")),
        name="mla_up",
    )(lat3, cos, sin, qg, kvg, wqn, wqr, wqrot, wkn, wv)


def _attn_kernel(qt_ref, k_ref, vt_ref, o_ref, m_ref, acc_ref, *, unroll):
    n_chunks, _, tk = vt_ref.shape
    n_sub = qt_ref.shape[1] // ATTN_QSUB
    m_ref[...] = jnp.full_like(m_ref, -jnp.inf)
    acc_ref[...] = jnp.zeros_like(acc_ref)

    def scores(c, j):
        rows = pl.ds(pl.multiple_of(c * tk, tk), tk)
        return _bdot(k_ref[rows, :], qt_ref[:, j * ATTN_QSUB:(j + 1) * ATTN_QSUB])

    def group(g, carry):
        chains = [(g * unroll + u, j) for u in range(unroll) for j in range(n_sub)]
        pending = [scores(*ch) for ch in chains[:ATTN_LOOKAHEAD]]
        for idx, (c, j) in enumerate(chains):
            cols = slice(j * ATTN_QSUB, (j + 1) * ATTN_QSUB)
            s = pending.pop(0)
            if idx + ATTN_LOOKAHEAD < len(chains):
                pending.append(scores(*chains[idx + ATTN_LOOKAHEAD]))
            m_old = m_ref[:, cols]
            m_new = jnp.maximum(m_old, jnp.max(s, axis=0, keepdims=True))
            alpha = jnp.exp2(m_old - m_new)
            p = jnp.exp2(s - m_new)
            acc_ref[:, cols] = alpha * acc_ref[:, cols] + _bdot(vt_ref[c], p.astype(jnp.bfloat16))
            m_ref[:, cols] = m_new
        return carry

    n_groups = n_chunks // unroll
    if n_groups == 1:
        group(0, 0)
    else:
        lax.fori_loop(0, n_groups, group, 0)
    o_ref[...] = (acc_ref[:MLA_V, :] / acc_ref[MLA_V:MLA_V + 1, :]).T.astype(o_ref.dtype)


def _attn(qt, kf, vt, tq, unroll):
    b, hh, _, s = qt.shape
    n_chunks, _, tk = vt.shape[2:]
    return pl.pallas_call(
        functools.partial(_attn_kernel, unroll=unroll),
        grid=(b, hh, s // tq),
        in_specs=[pl.BlockSpec((None, None, MLA_QK, tq), lambda i, h, q: (i, h, 0, q)),
                  pl.BlockSpec((None, None, s, MLA_QK), lambda i, h, q: (i, h, 0, 0)),
                  pl.BlockSpec((None, None, n_chunks, ATTN_VROWS, tk),
                               lambda i, h, q: (i, h, 0, 0, 0))],
        out_specs=pl.BlockSpec((None, tq, MLA_V), lambda i, h, q: (i, q, h)),
        out_shape=jax.ShapeDtypeStruct((b, s, MLA_V_W), jnp.bfloat16),
        scratch_shapes=[pltpu.VMEM((1, tq), jnp.float32),
                        pltpu.VMEM((ATTN_VROWS, tq), jnp.float32)],
        compiler_params=_cparams(("parallel", "parallel", "parallel")),
        name="attn",
    )(qt, kf, vt)


def _split2(x):
    hi = x.astype(jnp.bfloat16)
    lo = (x - hi.astype(jnp.float32)).astype(jnp.bfloat16)
    return hi, lo


def _gla_chunk_local(qkv_ref, g, rows, reverse):
    c = GLA_KERNEL_CHUNK
    row = lax.broadcasted_iota(jnp.int32, (c, c), 0)
    col = lax.broadcasted_iota(jnp.int32, (c, c), 1)
    if reverse:
        tri = (col >= row).astype(jnp.bfloat16)
        keep = col > row
        last = 0
    else:
        tri = (col <= row).astype(jnp.bfloat16)
        keep = col <= row
        last = c - 1
    mid = c // 2
    g_hi, g_lo = _split2(g)
    b = _bdot(jnp.concatenate([tri, tri], axis=1),
              jnp.concatenate([g_hi, g_lo], axis=0))
    b_mid = b[mid:mid + 1, :]
    b_last = b[last:last + 1, :]
    d = b - b_mid
    e_q = jnp.exp2(jnp.minimum(d, GLA_EXP2_CLAMP))
    e_k = jnp.exp2(jnp.minimum(-d, GLA_EXP2_CLAMP))
    q = qkv_ref[rows, 0:GLA_QK_W].astype(jnp.float32)
    k = qkv_ref[rows, GLA_QK_W:2 * GLA_QK_W].astype(jnp.float32)
    q_in = (q * e_q).astype(jnp.bfloat16)
    k_in = (k * e_k).astype(jnp.bfloat16)
    q_st = (q * jnp.exp2(b)).astype(jnp.bfloat16)
    k_st = (k * jnp.exp2(b_last - b)).astype(jnp.bfloat16)
    e_last = jnp.exp2(b_last)
    zeros = jnp.zeros((c, GLA_DK), jnp.bfloat16)
    keep2 = jnp.concatenate([keep, keep], axis=1)
    scores = []
    for hp in range(GLA_HEADS // 2):
        k0 = k_in[:, (2 * hp) * GLA_DK:(2 * hp + 1) * GLA_DK]
        k1 = k_in[:, (2 * hp + 1) * GLA_DK:(2 * hp + 2) * GLA_DK]
        k_diag = jnp.concatenate([jnp.concatenate([k0, zeros], axis=1),
                                  jnp.concatenate([zeros, k1], axis=1)], axis=0)
        a2 = _bdot_nt(q_in[:, 2 * hp * GLA_DK:(2 * hp + 2) * GLA_DK], k_diag)
        a2 = jnp.where(keep2, a2, 0.0).astype(jnp.bfloat16)
        scores += [a2[:, :c], a2[:, c:]]
    heads = []
    for h in range(GLA_HEADS):
        ks = slice(h * GLA_DK, (h + 1) * GLA_DK)
        v = qkv_ref[rows, 2 * GLA_QK_W + h * GLA_DV:2 * GLA_QK_W + (h + 1) * GLA_DV]
        upd = _bdot_tn(k_st[:, ks], v)
        decay_col = jnp.broadcast_to(e_last[:, ks], (GLA_DK, GLA_DK)).T
        decay = jnp.concatenate([decay_col] * (GLA_DV // GLA_DK), axis=1)
        heads.append((jnp.concatenate([scores[h], q_st[:, ks]], axis=1), v, upd, decay))
    return heads


def _gla_chunk_state(heads, rows, o_ref, state_ref):
    for h, (lhs, v, upd, decay) in enumerate(heads):
        st = state_ref[h]
        o = _bdot(lhs, jnp.concatenate([v, st.astype(jnp.bfloat16)], axis=0))
        o_ref[rows, h * GLA_DV:(h + 1) * GLA_DV] = o.astype(o_ref.dtype)
        state_ref[h] = st * decay + upd


def _gla_kernel(qkv_f_ref, g_f_ref, qkv_b_ref, g_b_ref, of_ref, ob_ref, sf_ref, sb_ref):
    @pl.when(pl.program_id(1) == 0)
    def _():
        sf_ref[...] = jnp.zeros_like(sf_ref)
        sb_ref[...] = jnp.zeros_like(sb_ref)

    c = GLA_KERNEL_CHUNK
    n_chunks = qkv_f_ref.shape[0] // c
    for ci in range(n_chunks):
        rows_f = slice(ci * c, (ci + 1) * c)
        rows_b = slice((n_chunks - 1 - ci) * c, (n_chunks - ci) * c)
        local_f = _gla_chunk_local(qkv_f_ref, g_f_ref[rows_f, :], rows_f, reverse=False)
        local_b = _gla_chunk_local(qkv_b_ref, g_b_ref[rows_b, :], rows_b, reverse=True)
        _gla_chunk_state(local_f, rows_f, of_ref, sf_ref)
        _gla_chunk_state(local_b, rows_b, ob_ref, sb_ref)


def _gla(qkv3, gates3, tb):
    b, s, _ = qkv3.shape
    nb = s // tb
    fwd = lambda i, j: (i, j, 0)
    bwd = lambda i, j: (i, nb - 1 - j, 0)
    return pl.pallas_call(
        _gla_kernel,
        grid=(b, nb),
        in_specs=[pl.BlockSpec((None, tb, QKV_W), fwd),
                  pl.BlockSpec((None, tb, GLA_QK_W), fwd),
                  pl.BlockSpec((None, tb, QKV_W), bwd),
                  pl.BlockSpec((None, tb, GLA_QK_W), lambda i, j: (i, nb - 1 - j, 1))],
        out_specs=[pl.BlockSpec((None, tb, GLA_V_W), fwd),
                   pl.BlockSpec((None, tb, GLA_V_W), bwd)],
        out_shape=[jax.ShapeDtypeStruct((b, s, GLA_V_W), jnp.bfloat16),
                   jax.ShapeDtypeStruct((b, s, GLA_V_W), jnp.bfloat16)],
        scratch_shapes=[pltpu.VMEM((GLA_HEADS, GLA_DK, GLA_DV), jnp.float32),
                        pltpu.VMEM((GLA_HEADS, GLA_DK, GLA_DV), jnp.float32)],
        compiler_params=_cparams(("parallel", "arbitrary")),
        name="gla",
    )(qkv3, gates3, qkv3, gates3)


def _final_kernel(x_ref, of_ref, ob_ref, om_ref, act_ref, gg_ref, pg_ref, wa_ref, wb_ref, wo_ref,
                  y_ref):
    o_sum = of_ref[...].astype(jnp.float32) + ob_ref[...].astype(jnp.float32)
    gg = gg_ref[...]
    o_a = jnp.concatenate(
        [_rms(o_sum[:, h * GLA_DV:(h + 1) * GLA_DV], gg) for h in range(GLA_HEADS)], axis=1)
    y_a = (o_a * act_ref[:, 0:D_MODEL].astype(jnp.float32)).astype(jnp.bfloat16)
    y_b = om_ref[...] * act_ref[:, D_MODEL:2 * D_MODEL]
    merged = (act_ref[:, 2 * D_MODEL:3 * D_MODEL].astype(jnp.float32) * _bdot(y_a, wa_ref[...])
              + act_ref[:, 3 * D_MODEL:].astype(jnp.float32) * _bdot(y_b, wb_ref[...]))
    out = _bdot(merged.astype(jnp.bfloat16), wo_ref[...])
    y_ref[...] = x_ref[...] + _rms(out, pg_ref[...])


def _final(x2d, o_f, o_b, o_m, act, gg, pg, wa, wb, wo, ts):
    t = x2d.shape[0]
    tok = lambda w: pl.BlockSpec((ts, w), lambda i: (i, 0))
    return pl.pallas_call(
        _final_kernel,
        grid=(t // ts,),
        in_specs=[tok(D_MODEL), tok(GLA_V_W), tok(GLA_V_W), tok(MLA_V_W), tok(ACT_W),
                  _const_spec((1, GLA_DV)), _const_spec((1, D_MODEL)),
                  _const_spec((GLA_V_W, D_MODEL)), _const_spec((MLA_V_W, D_MODEL)),
                  _const_spec((D_MODEL, D_MODEL))],
        out_specs=tok(D_MODEL),
        out_shape=jax.ShapeDtypeStruct((t, D_MODEL), jnp.float32),
        compiler_params=_cparams(("parallel",)),
        name="final",
    )(x2d, o_f, o_b, o_m, act, gg, pg, wa, wb, wo)


def _rotate_half_cols(w):
    half = MLA_ROPE // 2
    w3 = w.reshape(w.shape[0], -1, MLA_ROPE)
    return jnp.concatenate([-w3[..., half:], w3[..., :half]], axis=-1).reshape(w.shape)


def _pad_cols(w, width):
    return jnp.pad(w, ((0, 0), (0, width - w.shape[1])))


def _prepare_weights(pre_g, w_in, wg2_f, bg2_f, wg2_b, bg2_b, gla_g, q_g, w_uq, kv_g, w_ukv,
                     w_a, w_b, w_o, post_g):
    bf = jnp.bfloat16
    sizes = (GLA_QK_W, GLA_QK_W, GLA_V_W, GLA_GATE_RANK, GLA_GATE_RANK, GLA_V_W,
             MLA_Q_RANK, MLA_KV_RANK, MLA_ROPE, MLA_V_W, D_MODEL, D_MODEL)
    offs = [0]
    for sz in sizes:
        offs.append(offs[-1] + sz)
    (w_q, w_k, w_v, w_grf, w_grb, w_za, w_cq, w_ckv, w_kr, w_zb, w_ga, w_gb) = [
        w_in[:, offs[i]:offs[i + 1]] for i in range(len(sizes))]
    wqkv = jnp.concatenate([w_q, w_k, w_v], axis=1).astype(bf)
    wlat = jnp.concatenate([
        w_cq, w_ckv, _pad_cols(w_kr, V7X_LANES), _pad_cols(_rotate_half_cols(w_kr), V7X_LANES),
        _pad_cols(jnp.concatenate([w_grf, w_grb], axis=1), V7X_LANES)], axis=1).astype(bf)
    wact = jnp.concatenate([w_za, w_zb, w_ga, w_gb], axis=1).astype(bf)

    wg = jnp.zeros((V7X_LANES, GATES_W), jnp.float32)
    wg = wg.at[0:GLA_GATE_RANK, :GLA_QK_W].set(wg2_f)
    wg = wg.at[GLA_GATE_RANK:2 * GLA_GATE_RANK, GLA_QK_W:].set(wg2_b).astype(bf)
    bg = jnp.concatenate([bg2_f, bg2_b])[None, :]

    uq = w_uq.reshape(MLA_Q_RANK, MLA_HEADS, MLA_QK)
    wqn = uq[:, :, :MLA_NOPE].reshape(MLA_Q_RANK, MLA_HEADS * MLA_NOPE)
    wqr = uq[:, :, MLA_NOPE:].reshape(MLA_Q_RANK, MLA_HEADS * MLA_ROPE)
    ukv = w_ukv.reshape(MLA_KV_RANK, MLA_HEADS, MLA_NOPE + MLA_V)
    wkn = ukv[:, :, :MLA_NOPE].reshape(MLA_KV_RANK, MLA_HEADS * MLA_NOPE)
    wv = ukv[:, :, MLA_NOPE:].reshape(MLA_KV_RANK, MLA_HEADS * MLA_V)
    return dict(
        pre_g=pre_g[None, :], wqkv=wqkv, wlat=wlat, wact=wact,
        wg=wg, bg=bg,
        gla_g=gla_g[None, :], q_g=q_g[None, :], kv_g=kv_g[None, :],
        wqn=wqn.astype(bf), wqr=wqr.astype(bf), wqrot=_rotate_half_cols(wqr).astype(bf),
        wkn=wkn.astype(bf), wv=wv.astype(bf),
        w_a=w_a.astype(bf), w_b=w_b.astype(bf), w_o=w_o.astype(bf), post_g=post_g[None, :])


def _rope_tables(s):
    half = MLA_ROPE // 2
    freqs = ROPE_THETA ** (-jnp.arange(half, dtype=jnp.float32) / half)
    ang = jnp.arange(s).astype(jnp.float32)[:, None] * freqs[None, :]
    cos = jnp.cos(ang)
    sin = jnp.sin(ang)
    return jnp.concatenate([cos, cos], axis=1), jnp.concatenate([sin, sin], axis=1)


def _tiles(b, s):
    t = b * s
    tk = min(512, s)
    return dict(ts=min(512, t), tk=tk, tb=min(512, s), tq=min(2048, s),
                attn_unroll=min(8, s // tk))


def _layer(x, w):
    b, s, _ = x.shape
    t = b * s
    tl = _tiles(b, s)
    x2d = x.reshape(t, D_MODEL)
    qkv, lat, act, gates = _proj(x2d, w["pre_g"], w["wqkv"], w["wlat"], w["wact"], w["wg"], w["bg"],
                                 tl["ts"])
    lat3 = lat.reshape(b, s, LAT_W)
    cos, sin = _rope_tables(s)
    qt, kf, vt = _mla_up(lat3, cos, sin, w["q_g"], w["kv_g"], w["wqn"], w["wqr"], w["wqrot"],
                         w["wkn"], w["wv"], tl["tk"])
    o_m = _attn(qt, kf, vt, tl["tq"], tl["attn_unroll"])
    o_f, o_b = _gla(qkv.reshape(b, s, QKV_W), gates.reshape(b, s, GATES_W), tl["tb"])
    y = _final(x2d, o_f.reshape(t, GLA_V_W), o_b.reshape(t, GLA_V_W), o_m.reshape(t, MLA_V_W),
               act, w["gla_g"], w["post_g"], w["w_a"], w["w_b"], w["w_o"], tl["ts"])
    return y.reshape(b, s, D_MODEL)


def kernel(x_prompt, x_sample, pre_norm_g, w_in, gla_wg2_fwd, gla_bg2_fwd, gla_wg2_bwd, gla_bg2_bwd,
           gla_out_norm_g, mla_q_norm_g, mla_w_uq, mla_kv_norm_g, mla_w_ukv, w_branch_a, w_branch_b,
           w_out, post_norm_g):
    depth = pre_norm_g.shape[0]
    y_prompt, y_sample = x_prompt, x_sample
    for l in range(depth):
        w = _prepare_weights(pre_norm_g[l], w_in[l], gla_wg2_fwd[l], gla_bg2_fwd[l], gla_wg2_bwd[l],
                             gla_bg2_bwd[l], gla_out_norm_g[l], mla_q_norm_g[l], mla_w_uq[l],
                             mla_kv_norm_g[l], mla_w_ukv[l], w_branch_a[l], w_branch_b[l],
                             w_out[l], post_norm_g[l])
        y_prompt = _layer(y_prompt, w)
        y_sample = _layer(y_sample, w)
    return (y_prompt, y_sample)
```

```python
import functools
import math

import jax
import jax.numpy as jnp
from jax import lax
from jax.experimental import pallas as pl
from jax.experimental.pallas import tpu as pltpu

D_MODEL = 1024
GLA_HEADS = 4
GLA_DK = 128
GLA_DV = 256
GLA_GATE_RANK = 16
GLA_GATE_NORM = 16.0
MLA_HEADS = 8
MLA_Q_RANK = 256
MLA_KV_RANK = 128
MLA_NOPE = 128
MLA_ROPE = 64
MLA_V = 128
ROPE_THETA = 10000.0
EPS = 1e-6

GLA_QK_W = GLA_HEADS * GLA_DK
GLA_V_W = GLA_HEADS * GLA_DV
MLA_V_W = MLA_HEADS * MLA_V
MLA_QK = MLA_NOPE + MLA_ROPE

V7X_LANES = 128
V7X_VMEM_LIMIT_BYTES = 56 * 1024 * 1024

QKV_W = 2 * GLA_QK_W + GLA_V_W
LAT_CQ = 0
LAT_CKV = MLA_Q_RANK
LAT_KR = LAT_CKV + MLA_KV_RANK
LAT_KROT = LAT_KR + V7X_LANES
LAT_GR = LAT_KROT + V7X_LANES
LAT_W = LAT_GR + V7X_LANES
ACT_W = 4 * D_MODEL
GATES_W = 2 * GLA_QK_W

GLA_KERNEL_CHUNK = 128
GLA_LOOKAHEAD = 4
GLA_EXP2_CLAMP = 115.0
LOG2E = 1.4426950408889634
ATTN_QSUB = 256
ATTN_LOOKAHEAD = 3
ATTN_VROWS = MLA_V + 16


def _cparams(semantics):
    return pltpu.CompilerParams(dimension_semantics=semantics,
                                vmem_limit_bytes=V7X_VMEM_LIMIT_BYTES)


def _const_spec(shape):
    return pl.BlockSpec(shape, lambda *_: (0,) * len(shape))


def _rms(x, g):
    return x * lax.rsqrt(jnp.mean(x * x, axis=-1, keepdims=True) + EPS) * g


def _sigmoid(x):
    return 1.0 / (1.0 + jnp.exp(-x))


def _bdot(a, b):
    return jnp.dot(a, b, preferred_element_type=jnp.float32)


def _bdot_nt(a, b):
    return lax.dot_general(a, b, (((1,), (1,)), ((), ())), preferred_element_type=jnp.float32)


def _bdot_tn(a, b):
    return lax.dot_general(a, b, (((0,), (0,)), ((), ())), preferred_element_type=jnp.float32)


def _proj_kernel(x_ref, g_ref, wqkv_ref, wlat_ref, wact_ref, wg_ref, bg_ref,
                 qkv_ref, lat_ref, act_ref, gates_ref):
    half = x_ref.shape[0] // 2
    for r in (slice(0, half), slice(half, 2 * half)):
        h = _rms(x_ref[r, :], g_ref[...]).astype(jnp.bfloat16)
        lat = _bdot(h, wlat_ref[...])
        xg = _bdot(lat[:, LAT_GR:].astype(jnp.bfloat16), wg_ref[...]) + bg_ref[...]
        qkv = _bdot(h, wqkv_ref[...])
        lat_ref[r, :] = lat[:, :LAT_GR]
        gates_ref[r, :] = ((jnp.minimum(xg, 0.0) - jnp.log(1.0 + jnp.exp(-jnp.abs(xg))))
                           * (LOG2E / GLA_GATE_NORM))
        z = _bdot(h, wact_ref[:, :2 * D_MODEL])
        qkv_ref[r, :GLA_QK_W] = (qkv[:, :GLA_QK_W] * (GLA_DK ** -0.5)).astype(jnp.bfloat16)
        qkv_ref[r, GLA_QK_W:] = qkv[:, GLA_QK_W:].astype(jnp.bfloat16)
        gate = _bdot(h, wact_ref[:, 2 * D_MODEL:])
        act_ref[r, :2 * D_MODEL] = (z * _sigmoid(z)).astype(jnp.bfloat16)
        act_ref[r, 2 * D_MODEL:] = _sigmoid(gate).astype(jnp.bfloat16)


def _proj(x2d, pre_g, wqkv, wlat, wact, wg, bg, ts):
    t = x2d.shape[0]
    return pl.pallas_call(
        _proj_kernel,
        grid=(t // ts,),
        in_specs=[pl.BlockSpec((ts, D_MODEL), lambda i: (i, 0)),
                  _const_spec((1, D_MODEL)),
                  _const_spec((D_MODEL, QKV_W)),
                  _const_spec((D_MODEL, LAT_W)),
                  _const_spec((D_MODEL, ACT_W)),
                  _const_spec((V7X_LANES, GATES_W)),
                  _const_spec((1, GATES_W))],
        out_specs=[pl.BlockSpec((ts, QKV_W), lambda i: (i, 0)),
                   pl.BlockSpec((ts, LAT_GR), lambda i: (i, 0)),
                   pl.BlockSpec((ts, ACT_W), lambda i: (i, 0)),
                   pl.BlockSpec((ts, GATES_W), lambda i: (i, 0))],
        out_shape=[jax.ShapeDtypeStruct((t, QKV_W), jnp.bfloat16),
                   jax.ShapeDtypeStruct((t, LAT_GR), jnp.float32),
                   jax.ShapeDtypeStruct((t, ACT_W), jnp.bfloat16),
                   jax.ShapeDtypeStruct((t, GATES_W), jnp.float32)],
        compiler_params=_cparams(("parallel",)),
        name="proj",
    )(x2d, pre_g, wqkv, wlat, wact, wg, bg)


def _mla_up_kernel(lat_ref, cos_ref, sin_ref, qg_ref, kvg_ref, wqn_ref, wqr_ref, wqrot_ref,
                   wkn_ref, wv_ref, qt_ref, kf_ref, vt_ref):
    lat = lat_ref[...]
    cos = cos_ref[...]
    sin = sin_ref[...]
    c_q = _rms(lat[:, LAT_CQ:LAT_CQ + MLA_Q_RANK], qg_ref[...]).astype(jnp.bfloat16)
    c_kv = _rms(lat[:, LAT_CKV:LAT_CKV + MLA_KV_RANK], kvg_ref[...]).astype(jnp.bfloat16)

    q_scale = (MLA_QK ** -0.5) * LOG2E
    qn_t = (_bdot(c_q, wqn_ref[...]) * q_scale).T
    cos_h = jnp.concatenate([cos, cos], axis=1)
    sin_h = jnp.concatenate([sin, sin], axis=1)
    cos_q = jnp.concatenate([cos_h] * (MLA_HEADS // 2), axis=1)
    sin_q = jnp.concatenate([sin_h] * (MLA_HEADS // 2), axis=1)
    q_pe = _bdot(c_q, wqr_ref[...]) * cos_q + _bdot(c_q, wqrot_ref[...]) * sin_q
    qpe_t = (q_pe * q_scale).T

    k_pe = (lat[:, LAT_KR:LAT_KR + MLA_ROPE] * cos
            + lat[:, LAT_KROT:LAT_KROT + MLA_ROPE] * sin).astype(jnp.bfloat16)
    kn = _bdot(c_kv, wkn_ref[...]).astype(jnp.bfloat16)
    v_t = _bdot(c_kv, wv_ref[...]).T

    n_kc, tk = vt_ref.shape[1], vt_ref.shape[3]
    tile_row = lax.broadcasted_iota(jnp.int32, (ATTN_VROWS - MLA_V, tk), 0)
    ones_tile = (tile_row == 0).astype(jnp.bfloat16)
    for h in range(MLA_HEADS):
        qt_ref[h, :MLA_NOPE, :] = qn_t[h * MLA_NOPE:(h + 1) * MLA_NOPE, :].astype(jnp.bfloat16)
        qt_ref[h, MLA_NOPE:, :] = qpe_t[h * MLA_ROPE:(h + 1) * MLA_ROPE, :].astype(jnp.bfloat16)
        kf_ref[h, :, :MLA_NOPE] = kn[:, h * MLA_NOPE:(h + 1) * MLA_NOPE]
        kf_ref[h, :, MLA_NOPE:] = k_pe
        for u in range(n_kc):
            vt_ref[h, u, :MLA_V, :] = v_t[h * MLA_V:(h + 1) * MLA_V,
                                          u * tk:(u + 1) * tk].astype(jnp.bfloat16)
            vt_ref[h, u, MLA_V:, :] = ones_tile


def _mla_up(lat3, cos, sin, qg, kvg, wqn, wqr, wqrot, wkn, wv, ts, tk):
    b, s, _ = lat3.shape
    hh = MLA_HEADS
    return pl.pallas_call(
        _mla_up_kernel,
        grid=(b, s // ts),
        in_specs=[pl.BlockSpec((None, ts, LAT_GR), lambda i, j: (i, j, 0)),
                  pl.BlockSpec((ts, MLA_ROPE), lambda i, j: (j, 0)),
                  pl.BlockSpec((ts, MLA_ROPE), lambda i, j: (j, 0)),
                  _const_spec((1, MLA_Q_RANK)),
                  _const_spec((1, MLA_KV_RANK)),
                  _const_spec((MLA_Q_RANK, hh * MLA_NOPE)),
                  _const_spec((MLA_Q_RANK, hh * MLA_ROPE)),
                  _const_spec((MLA_Q_RANK, hh * MLA_ROPE)),
                  _const_spec((MLA_KV_RANK, hh * MLA_NOPE)),
                  _const_spec((MLA_KV_RANK, hh * MLA_V))],
        out_specs=[pl.BlockSpec((None, hh, MLA_QK, ts), lambda i, j: (i, 0, 0, j)),
                   pl.BlockSpec((None, hh, ts, MLA_QK), lambda i, j: (i, 0, j, 0)),
                   pl.BlockSpec((None, hh, ts // tk, ATTN_VROWS, tk),
                                lambda i, j: (i, 0, j, 0, 0))],
        out_shape=[jax.ShapeDtypeStruct((b, hh, MLA_QK, s), jnp.bfloat16),
                   jax.ShapeDtypeStruct((b, hh, s, MLA_QK), jnp.bfloat16),
                   jax.ShapeDtypeStruct((b, hh, s // tk, ATTN_VROWS, tk), jnp.bfloat16)],
        compiler_params=_cparams(("parallel", "parallel")),
        name="mla_up",
    )(lat3, cos, sin, qg, kvg, wqn, wqr, wqrot, wkn, wv)


def _attn_kernel(qt_ref, k_ref, vt_ref, o_ref, m_ref, acc_ref, *, unroll):
    n_chunks, _, tk = vt_ref.shape
    n_sub = qt_ref.shape[1] // ATTN_QSUB
    m_ref[...] = jnp.full_like(m_ref, -jnp.inf)
    acc_ref[...] = jnp.zeros_like(acc_ref)

    def scores(c, j):
        rows = pl.ds(pl.multiple_of(c * tk, tk), tk)
        return _bdot(k_ref[rows, :], qt_ref[:, j * ATTN_QSUB:(j + 1) * ATTN_QSUB])

    def group(g, carry):
        chains = [(g * unroll + u, j) for u in range(unroll) for j in range(n_sub)]
        pending = [scores(*ch) for ch in chains[:ATTN_LOOKAHEAD]]
        for idx, (c, j) in enumerate(chains):
            cols = slice(j * ATTN_QSUB, (j + 1) * ATTN_QSUB)
            s = pending.pop(0)
            if idx + ATTN_LOOKAHEAD < len(chains):
                pending.append(scores(*chains[idx + ATTN_LOOKAHEAD]))
            m_old = m_ref[:, cols]
            m_new = jnp.maximum(m_old, jnp.max(s, axis=0, keepdims=True))
            alpha = jnp.exp2(m_old - m_new)
            p = jnp.exp2(s - m_new)
            acc_ref[:, cols] = alpha * acc_ref[:, cols] + _bdot(vt_ref[c], p.astype(jnp.bfloat16))
            m_ref[:, cols] = m_new
        return carry

    n_groups = n_chunks // unroll
    if n_groups == 1:
        group(0, 0)
    else:
        lax.fori_loop(0, n_groups, group, 0)
    o_ref[...] = (acc_ref[:MLA_V, :] / acc_ref[MLA_V:MLA_V + 1, :]).T.astype(o_ref.dtype)


def _attn(qt, kf, vt, tq, unroll):
    b, hh, _, s = qt.shape
    n_chunks, _, tk = vt.shape[2:]
    return pl.pallas_call(
        functools.partial(_attn_kernel, unroll=unroll),
        grid=(b, hh, s // tq),
        in_specs=[pl.BlockSpec((None, None, MLA_QK, tq), lambda i, h, q: (i, h, 0, q)),
                  pl.BlockSpec((None, None, s, MLA_QK), lambda i, h, q: (i, h, 0, 0)),
                  pl.BlockSpec((None, None, n_chunks, ATTN_VROWS, tk),
                               lambda i, h, q: (i, h, 0, 0, 0))],
        out_specs=pl.BlockSpec((None, tq, MLA_V), lambda i, h, q: (i, q, h)),
        out_shape=jax.ShapeDtypeStruct((b, s, MLA_V_W), jnp.bfloat16),
        scratch_shapes=[pltpu.VMEM((1, tq), jnp.float32),
                        pltpu.VMEM((ATTN_VROWS, tq), jnp.float32)],
        compiler_params=_cparams(("parallel", "parallel", "parallel")),
        name="attn",
    )(qt, kf, vt)


def _split2(x):
    hi = x.astype(jnp.bfloat16)
    lo = (x - hi.astype(jnp.float32)).astype(jnp.bfloat16)
    return hi, lo


def _gla_cumsum(g, reverse):
    c = GLA_KERNEL_CHUNK
    row = lax.broadcasted_iota(jnp.int32, (c, c), 0)
    col = lax.broadcasted_iota(jnp.int32, (c, c), 1)
    tri = ((col >= row) if reverse else (col <= row)).astype(jnp.bfloat16)
    g_hi, g_lo = _split2(g)
    return _bdot(jnp.concatenate([tri, tri], axis=1),
                 jnp.concatenate([g_hi, g_lo], axis=0))


def _gla_chunk_local(qkv_ref, b, rows, reverse):
    c = GLA_KERNEL_CHUNK
    row = lax.broadcasted_iota(jnp.int32, (c, c), 0)
    col = lax.broadcasted_iota(jnp.int32, (c, c), 1)
    keep = (col > row) if reverse else (col <= row)
    last = 0 if reverse else c - 1
    mid = c // 2
    b_mid = b[mid:mid + 1, :]
    b_last = b[last:last + 1, :]
    d = b - b_mid
    e_q = jnp.exp2(jnp.minimum(d, GLA_EXP2_CLAMP))
    e_k = jnp.exp2(jnp.minimum(-d, GLA_EXP2_CLAMP))
    q = qkv_ref[rows, 0:GLA_QK_W].astype(jnp.float32)
    k = qkv_ref[rows, GLA_QK_W:2 * GLA_QK_W].astype(jnp.float32)
    q_in = (q * e_q).astype(jnp.bfloat16)
    k_in = (k * e_k).astype(jnp.bfloat16)
    q_st = (q * jnp.exp2(b)).astype(jnp.bfloat16)
    k_st = (k * jnp.exp2(b_last - b)).astype(jnp.bfloat16)
    e_last = jnp.exp2(b_last)
    zeros = jnp.zeros((c, GLA_DK), jnp.bfloat16)
    keep2 = jnp.concatenate([keep, keep], axis=1)
    scores = []
    for hp in range(GLA_HEADS // 2):
        k0 = k_in[:, (2 * hp) * GLA_DK:(2 * hp + 1) * GLA_DK]
        k1 = k_in[:, (2 * hp + 1) * GLA_DK:(2 * hp + 2) * GLA_DK]
        k_diag = jnp.concatenate([jnp.concatenate([k0, zeros], axis=1),
                                  jnp.concatenate([zeros, k1], axis=1)], axis=0)
        a2 = _bdot_nt(q_in[:, 2 * hp * GLA_DK:(2 * hp + 2) * GLA_DK], k_diag)
        a2 = jnp.where(keep2, a2, 0.0).astype(jnp.bfloat16)
        scores += [a2[:, :c], a2[:, c:]]
    heads = []
    for h in range(GLA_HEADS):
        ks = slice(h * GLA_DK, (h + 1) * GLA_DK)
        v = qkv_ref[rows, 2 * GLA_QK_W + h * GLA_DV:2 * GLA_QK_W + (h + 1) * GLA_DV]
        upd = _bdot_tn(k_st[:, ks], v)
        decay_col = jnp.broadcast_to(e_last[:, ks], (GLA_DK, GLA_DK)).T
        decay = jnp.concatenate([decay_col] * (GLA_DV // GLA_DK), axis=1)
        heads.append((jnp.concatenate([scores[h], q_st[:, ks]], axis=1), v, upd, decay))
    return heads


def _gla_chunk_state(heads, rows, o_ref, state_ref):
    for h, (lhs, v, upd, decay) in enumerate(heads):
        st = state_ref[h]
        o = _bdot(lhs, jnp.concatenate([v, st.astype(jnp.bfloat16)], axis=0))
        o_ref[rows, h * GLA_DV:(h + 1) * GLA_DV] = o.astype(o_ref.dtype)
        state_ref[h] = st * decay + upd


def _gla_kernel(qkv_f_ref, g_f_ref, qkv_b_ref, g_b_ref, of_ref, ob_ref, sf_ref, sb_ref):
    @pl.when(pl.program_id(1) == 0)
    def _():
        sf_ref[...] = jnp.zeros_like(sf_ref)
        sb_ref[...] = jnp.zeros_like(sb_ref)

    c = GLA_KERNEL_CHUNK
    n_chunks = qkv_f_ref.shape[0] // c
    work = []
    for ci in range(n_chunks):
        rows_f = slice(ci * c, (ci + 1) * c)
        rows_b = slice((n_chunks - 1 - ci) * c, (n_chunks - ci) * c)
        work.append((qkv_f_ref, g_f_ref, rows_f, False, of_ref, sf_ref))
        work.append((qkv_b_ref, g_b_ref, rows_b, True, ob_ref, sb_ref))
    sums = [_gla_cumsum(g_ref[rows, :], rev) for (_, g_ref, rows, rev, _, _) in work]
    local = lambda i: _gla_chunk_local(work[i][0], sums[i], work[i][2], work[i][3])
    pending = [local(i) for i in range(min(GLA_LOOKAHEAD, len(work)))]
    for i, (_, _, rows, _, o_ref, state_ref) in enumerate(work):
        heads = pending.pop(0)
        if i + GLA_LOOKAHEAD < len(work):
            pending.append(local(i + GLA_LOOKAHEAD))
        _gla_chunk_state(heads, rows, o_ref, state_ref)


def _gla(qkv3, gates3, tb):
    b, s, _ = qkv3.shape
    nb = s // tb
    fwd = lambda i, j: (i, j, 0)
    bwd = lambda i, j: (i, nb - 1 - j, 0)
    return pl.pallas_call(
        _gla_kernel,
        grid=(b, nb),
        in_specs=[pl.BlockSpec((None, tb, QKV_W), fwd),
                  pl.BlockSpec((None, tb, GLA_QK_W), fwd),
                  pl.BlockSpec((None, tb, QKV_W), bwd),
                  pl.BlockSpec((None, tb, GLA_QK_W), lambda i, j: (i, nb - 1 - j, 1))],
        out_specs=[pl.BlockSpec((None, tb, GLA_V_W), fwd),
                   pl.BlockSpec((None, tb, GLA_V_W), bwd)],
        out_shape=[jax.ShapeDtypeStruct((b, s, GLA_V_W), jnp.bfloat16),
                   jax.ShapeDtypeStruct((b, s, GLA_V_W), jnp.bfloat16)],
        scratch_shapes=[pltpu.VMEM((GLA_HEADS, GLA_DK, GLA_DV), jnp.float32),
                        pltpu.VMEM((GLA_HEADS, GLA_DK, GLA_DV), jnp.float32)],
        compiler_params=_cparams(("parallel", "arbitrary")),
        name="gla",
    )(qkv3, gates3, qkv3, gates3)


def _final_kernel(x_ref, of_ref, ob_ref, om_ref, act_ref, gg_ref, pg_ref, wa_ref, wb_ref, wo_ref,
                  y_ref):
    o_sum = of_ref[...].astype(jnp.float32) + ob_ref[...].astype(jnp.float32)
    gg = gg_ref[...]
    o_a = jnp.concatenate(
        [_rms(o_sum[:, h * GLA_DV:(h + 1) * GLA_DV], gg) for h in range(GLA_HEADS)], axis=1)
    y_a = (o_a * act_ref[:, 0:D_MODEL].astype(jnp.float32)).astype(jnp.bfloat16)
    y_b = om_ref[...] * act_ref[:, D_MODEL:2 * D_MODEL]
    merged = (act_ref[:, 2 * D_MODEL:3 * D_MODEL].astype(jnp.float32) * _bdot(y_a, wa_ref[...])
              + act_ref[:, 3 * D_MODEL:].astype(jnp.float32) * _bdot(y_b, wb_ref[...]))
    out = _bdot(merged.astype(jnp.bfloat16), wo_ref[...])
    y_ref[...] = x_ref[...] + _rms(out, pg_ref[...])


def _final(x2d, o_f, o_b, o_m, act, gg, pg, wa, wb, wo, ts):
    t = x2d.shape[0]
    tok = lambda w: pl.BlockSpec((ts, w), lambda i: (i, 0))
    return pl.pallas_call(
        _final_kernel,
        grid=(t // ts,),
        in_specs=[tok(D_MODEL), tok(GLA_V_W), tok(GLA_V_W), tok(MLA_V_W), tok(ACT_W),
                  _const_spec((1, GLA_DV)), _const_spec((1, D_MODEL)),
                  _const_spec((GLA_V_W, D_MODEL)), _const_spec((MLA_V_W, D_MODEL)),
                  _const_spec((D_MODEL, D_MODEL))],
        out_specs=tok(D_MODEL),
        out_shape=jax.ShapeDtypeStruct((t, D_MODEL), jnp.float32),
        compiler_params=_cparams(("parallel",)),
        name="final",
    )(x2d, o_f, o_b, o_m, act, gg, pg, wa, wb, wo)


def _rotate_half_cols(w):
    half = MLA_ROPE // 2
    w3 = w.reshape(w.shape[0], -1, MLA_ROPE)
    return jnp.concatenate([-w3[..., half:], w3[..., :half]], axis=-1).reshape(w.shape)


def _pad_cols(w, width):
    return jnp.pad(w, ((0, 0), (0, width - w.shape[1])))


def _prepare_weights(pre_g, w_in, wg2_f, bg2_f, wg2_b, bg2_b, gla_g, q_g, w_uq, kv_g, w_ukv,
                     w_a, w_b, w_o, post_g):
    bf = jnp.bfloat16
    sizes = (GLA_QK_W, GLA_QK_W, GLA_V_W, GLA_GATE_RANK, GLA_GATE_RANK, GLA_V_W,
             MLA_Q_RANK, MLA_KV_RANK, MLA_ROPE, MLA_V_W, D_MODEL, D_MODEL)
    offs = [0]
    for sz in sizes:
        offs.append(offs[-1] + sz)
    (w_q, w_k, w_v, w_grf, w_grb, w_za, w_cq, w_ckv, w_kr, w_zb, w_ga, w_gb) = [
        w_in[:, offs[i]:offs[i + 1]] for i in range(len(sizes))]
    wqkv = jnp.concatenate([w_q, w_k, w_v], axis=1).astype(bf)
    wlat = jnp.concatenate([
        w_cq, w_ckv, _pad_cols(w_kr, V7X_LANES), _pad_cols(_rotate_half_cols(w_kr), V7X_LANES),
        _pad_cols(jnp.concatenate([w_grf, w_grb], axis=1), V7X_LANES)], axis=1).astype(bf)
    wact = jnp.concatenate([w_za, w_zb, w_ga, w_gb], axis=1).astype(bf)

    wg = jnp.zeros((V7X_LANES, GATES_W), jnp.float32)
    wg = wg.at[0:GLA_GATE_RANK, :GLA_QK_W].set(wg2_f)
    wg = wg.at[GLA_GATE_RANK:2 * GLA_GATE_RANK, GLA_QK_W:].set(wg2_b).astype(bf)
    bg = jnp.concatenate([bg2_f, bg2_b])[None, :]

    uq = w_uq.reshape(MLA_Q_RANK, MLA_HEADS, MLA_QK)
    wqn = uq[:, :, :MLA_NOPE].reshape(MLA_Q_RANK, MLA_HEADS * MLA_NOPE)
    wqr = uq[:, :, MLA_NOPE:].reshape(MLA_Q_RANK, MLA_HEADS * MLA_ROPE)
    ukv = w_ukv.reshape(MLA_KV_RANK, MLA_HEADS, MLA_NOPE + MLA_V)
    wkn = ukv[:, :, :MLA_NOPE].reshape(MLA_KV_RANK, MLA_HEADS * MLA_NOPE)
    wv = ukv[:, :, MLA_NOPE:].reshape(MLA_KV_RANK, MLA_HEADS * MLA_V)
    return dict(
        pre_g=pre_g[None, :], wqkv=wqkv, wlat=wlat, wact=wact,
        wg=wg, bg=bg,
        gla_g=gla_g[None, :], q_g=q_g[None, :], kv_g=kv_g[None, :],
        wqn=wqn.astype(bf), wqr=wqr.astype(bf), wqrot=_rotate_half_cols(wqr).astype(bf),
        wkn=wkn.astype(bf), wv=wv.astype(bf),
        w_a=w_a.astype(bf), w_b=w_b.astype(bf), w_o=w_o.astype(bf), post_g=post_g[None, :])


def _rope_tables(s):
    half = MLA_ROPE // 2
    freqs = ROPE_THETA ** (-jnp.arange(half, dtype=jnp.float32) / half)
    ang = jnp.arange(s).astype(jnp.float32)[:, None] * freqs[None, :]
    cos = jnp.cos(ang)
    sin = jnp.sin(ang)
    return jnp.concatenate([cos, cos], axis=1), jnp.concatenate([sin, sin], axis=1)


def _tiles(b, s):
    t = b * s
    tk = min(512, s)
    return dict(ts=min(512, t), ts_up=min(1024, s), tk=tk, tb=min(512, s), tq=min(2048, s),
                attn_unroll=min(8, s // tk))


def _layer(x, w):
    b, s, _ = x.shape
    t = b * s
    tl = _tiles(b, s)
    x2d = x.reshape(t, D_MODEL)
    qkv, lat, act, gates = _proj(x2d, w["pre_g"], w["wqkv"], w["wlat"], w["wact"], w["wg"], w["bg"],
                                 tl["ts"])
    lat3 = lat.reshape(b, s, LAT_GR)
    cos, sin = _rope_tables(s)
    qt, kf, vt = _mla_up(lat3, cos, sin, w["q_g"], w["kv_g"], w["wqn"], w["wqr"], w["wqrot"],
                         w["wkn"], w["wv"], tl["ts_up"], tl["tk"])
    o_m = _attn(qt, kf, vt, tl["tq"], tl["attn_unroll"])
    o_f, o_b = _gla(qkv.reshape(b, s, QKV_W), gates.reshape(b, s, GATES_W), tl["tb"])
    y = _final(x2d, o_f.reshape(t, GLA_V_W), o_b.reshape(t, GLA_V_W), o_m.reshape(t, MLA_V_W),
               act, w["gla_g"], w["post_g"], w["w_a"], w["w_b"], w["w_o"], tl["ts"])
    return y.reshape(b, s, D_MODEL)


def kernel(x_prompt, x_sample, pre_norm_g, w_in, gla_wg2_fwd, gla_bg2_fwd, gla_wg2_bwd, gla_bg2_bwd,
           gla_out_norm_g, mla_q_norm_g, mla_w_uq, mla_kv_norm_g, mla_w_ukv, w_branch_a, w_branch_b,
           w_out, post_norm_g):
    depth = pre_norm_g.shape[0]
    y_prompt, y_sample = x_prompt, x_sample
    for l in range(depth):
        w = _prepare_weights(pre_norm_g[l], w_in[l], gla_wg2_fwd[l], gla_bg2_fwd[l], gla_wg2_bwd[l],
                             gla_bg2_bwd[l], gla_out_norm_g[l], mla_q_norm_g[l], mla_w_uq[l],
                             mla_kv_norm_g[l], mla_w_ukv[l], w_branch_a[l], w_branch_b[l],
                             w_out[l], post_norm_g[l])
        y_prompt = _layer(y_prompt, w)
        y_sample = _layer(y_sample, w)
    return (y_prompt, y_sample)
```

```python
import functools
import math

import jax
import jax.numpy as jnp
from jax import lax
from jax.experimental import pallas as pl
from jax.experimental.pallas import tpu as pltpu

D_MODEL = 1024
GLA_HEADS = 4
GLA_DK = 128
GLA_DV = 256
GLA_GATE_RANK = 16
GLA_GATE_NORM = 16.0
MLA_HEADS = 8
MLA_Q_RANK = 256
MLA_KV_RANK = 128
MLA_NOPE = 128
MLA_ROPE = 64
MLA_V = 128
ROPE_THETA = 10000.0
EPS = 1e-6

GLA_QK_W = GLA_HEADS * GLA_DK
GLA_V_W = GLA_HEADS * GLA_DV
MLA_V_W = MLA_HEADS * MLA_V
MLA_QK = MLA_NOPE + MLA_ROPE

V7X_LANES = 128
V7X_VMEM_LIMIT_BYTES = 56 * 1024 * 1024

QKV_W = 2 * GLA_QK_W + GLA_V_W
LAT_CQ = 0
LAT_CKV = MLA_Q_RANK
LAT_KR = LAT_CKV + MLA_KV_RANK
LAT_KROT = LAT_KR + V7X_LANES
LAT_GR = LAT_KROT + V7X_LANES
LAT_W = LAT_GR + V7X_LANES
ACT_W = 4 * D_MODEL
GATES_W = 2 * GLA_QK_W

GLA_KERNEL_CHUNK = 128
GLA_LOOKAHEAD = 4
GLA_EXP2_CLAMP = 115.0
LOG2E = 1.4426950408889634
ATTN_QSUB = 256
ATTN_LOOKAHEAD = 3
ATTN_VROWS = MLA_V + 16


def _cparams(semantics):
    return pltpu.CompilerParams(dimension_semantics=semantics,
                                vmem_limit_bytes=V7X_VMEM_LIMIT_BYTES)


def _const_spec(shape):
    return pl.BlockSpec(shape, lambda *_: (0,) * len(shape))


def _rms(x, g):
    return x * lax.rsqrt(jnp.mean(x * x, axis=-1, keepdims=True) + EPS) * g


def _sigmoid(x):
    return 1.0 / (1.0 + jnp.exp(-x))


def _bdot(a, b):
    return jnp.dot(a, b, preferred_element_type=jnp.float32)


def _bdot_nt(a, b):
    return lax.dot_general(a, b, (((1,), (1,)), ((), ())), preferred_element_type=jnp.float32)


def _bdot_tn(a, b):
    return lax.dot_general(a, b, (((0,), (0,)), ((), ())), preferred_element_type=jnp.float32)


def _proj_kernel(x_ref, g_ref, wqkv_ref, wlat_ref, wact_ref, wg_ref, bg_ref,
                 qkv_ref, lat_ref, act_ref, gates_ref):
    half = x_ref.shape[0] // 2
    for r in (slice(0, half), slice(half, 2 * half)):
        h = _rms(x_ref[r, :], g_ref[...]).astype(jnp.bfloat16)
        lat = _bdot(h, wlat_ref[...])
        xg = _bdot(lat[:, LAT_GR:].astype(jnp.bfloat16), wg_ref[...]) + bg_ref[...]
        qkv = _bdot(h, wqkv_ref[...])
        lat_ref[r, :] = lat[:, :LAT_GR]
        gates_ref[r, :] = ((jnp.minimum(xg, 0.0) - jnp.log(1.0 + jnp.exp(-jnp.abs(xg))))
                           * (LOG2E / GLA_GATE_NORM))
        z = _bdot(h, wact_ref[:, :2 * D_MODEL])
        qkv_ref[r, :GLA_QK_W] = (qkv[:, :GLA_QK_W] * (GLA_DK ** -0.5)).astype(jnp.bfloat16)
        qkv_ref[r, GLA_QK_W:] = qkv[:, GLA_QK_W:].astype(jnp.bfloat16)
        gate = _bdot(h, wact_ref[:, 2 * D_MODEL:])
        act_ref[r, :2 * D_MODEL] = (z * _sigmoid(z)).astype(jnp.bfloat16)
        act_ref[r, 2 * D_MODEL:] = _sigmoid(gate).astype(jnp.bfloat16)


def _proj(x2d, pre_g, wqkv, wlat, wact, wg, bg, ts):
    t = x2d.shape[0]
    return pl.pallas_call(
        _proj_kernel,
        grid=(t // ts,),
        in_specs=[pl.BlockSpec((ts, D_MODEL), lambda i: (i, 0)),
                  _const_spec((1, D_MODEL)),
                  _const_spec((D_MODEL, QKV_W)),
                  _const_spec((D_MODEL, LAT_W)),
                  _const_spec((D_MODEL, ACT_W)),
                  _const_spec((V7X_LANES, GATES_W)),
                  _const_spec((1, GATES_W))],
        out_specs=[pl.BlockSpec((ts, QKV_W), lambda i: (i, 0)),
                   pl.BlockSpec((ts, LAT_GR), lambda i: (i, 0)),
                   pl.BlockSpec((ts, ACT_W), lambda i: (i, 0)),
                   pl.BlockSpec((ts, GATES_W), lambda i: (i, 0))],
        out_shape=[jax.ShapeDtypeStruct((t, QKV_W), jnp.bfloat16),
                   jax.ShapeDtypeStruct((t, LAT_GR), jnp.float32),
                   jax.ShapeDtypeStruct((t, ACT_W), jnp.bfloat16),
                   jax.ShapeDtypeStruct((t, GATES_W), jnp.float32)],
        compiler_params=_cparams(("parallel",)),
        name="proj",
    )(x2d, pre_g, wqkv, wlat, wact, wg, bg)


def _mla_up_kernel(lat_ref, cos_ref, sin_ref, qg_ref, kvg_ref, wqn_ref, wqr_ref, wqrot_ref,
                   wkn_ref, wv_ref, qt_ref, kf_ref, vt_ref):
    lat = lat_ref[...]
    cos = cos_ref[...]
    sin = sin_ref[...]
    c_q = _rms(lat[:, LAT_CQ:LAT_CQ + MLA_Q_RANK], qg_ref[...]).astype(jnp.bfloat16)
    c_kv = _rms(lat[:, LAT_CKV:LAT_CKV + MLA_KV_RANK], kvg_ref[...]).astype(jnp.bfloat16)

    q_scale = (MLA_QK ** -0.5) * LOG2E
    qn_t = (_bdot(c_q, wqn_ref[...]) * q_scale).T
    cos_h = jnp.concatenate([cos, cos], axis=1)
    sin_h = jnp.concatenate([sin, sin], axis=1)
    cos_q = jnp.concatenate([cos_h] * (MLA_HEADS // 2), axis=1)
    sin_q = jnp.concatenate([sin_h] * (MLA_HEADS // 2), axis=1)
    q_pe = _bdot(c_q, wqr_ref[...]) * cos_q + _bdot(c_q, wqrot_ref[...]) * sin_q
    qpe_t = (q_pe * q_scale).T

    k_pe = (lat[:, LAT_KR:LAT_KR + MLA_ROPE] * cos
            + lat[:, LAT_KROT:LAT_KROT + MLA_ROPE] * sin).astype(jnp.bfloat16)
    kn = _bdot(c_kv, wkn_ref[...]).astype(jnp.bfloat16)
    v_t = _bdot(c_kv, wv_ref[...]).T

    n_kc, tk = vt_ref.shape[1], vt_ref.shape[3]
    tile_row = lax.broadcasted_iota(jnp.int32, (ATTN_VROWS - MLA_V, tk), 0)
    ones_tile = (tile_row == 0).astype(jnp.bfloat16)
    for h in range(MLA_HEADS):
        qt_ref[h, :MLA_NOPE, :] = qn_t[h * MLA_NOPE:(h + 1) * MLA_NOPE, :].astype(jnp.bfloat16)
        qt_ref[h, MLA_NOPE:, :] = qpe_t[h * MLA_ROPE:(h + 1) * MLA_ROPE, :].astype(jnp.bfloat16)
        kf_ref[h, :, :MLA_NOPE] = kn[:, h * MLA_NOPE:(h + 1) * MLA_NOPE]
        kf_ref[h, :, MLA_NOPE:] = k_pe
        for u in range(n_kc):
            vt_ref[h, u, :MLA_V, :] = v_t[h * MLA_V:(h + 1) * MLA_V,
                                          u * tk:(u + 1) * tk].astype(jnp.bfloat16)
            vt_ref[h, u, MLA_V:, :] = ones_tile


def _mla_up(lat3, cos, sin, qg, kvg, wqn, wqr, wqrot, wkn, wv, ts, tk):
    b, s, _ = lat3.shape
    hh = MLA_HEADS
    return pl.pallas_call(
        _mla_up_kernel,
        grid=(b, s // ts),
        in_specs=[pl.BlockSpec((None, ts, LAT_GR), lambda i, j: (i, j, 0)),
                  pl.BlockSpec((ts, MLA_ROPE), lambda i, j: (j, 0)),
                  pl.BlockSpec((ts, MLA_ROPE), lambda i, j: (j, 0)),
                  _const_spec((1, MLA_Q_RANK)),
                  _const_spec((1, MLA_KV_RANK)),
                  _const_spec((MLA_Q_RANK, hh * MLA_NOPE)),
                  _const_spec((MLA_Q_RANK, hh * MLA_ROPE)),
                  _const_spec((MLA_Q_RANK, hh * MLA_ROPE)),
                  _const_spec((MLA_KV_RANK, hh * MLA_NOPE)),
                  _const_spec((MLA_KV_RANK, hh * MLA_V))],
        out_specs=[pl.BlockSpec((None, hh, MLA_QK, ts), lambda i, j: (i, 0, 0, j)),
                   pl.BlockSpec((None, hh, ts, MLA_QK), lambda i, j: (i, 0, j, 0)),
                   pl.BlockSpec((None, hh, ts // tk, ATTN_VROWS, tk),
                                lambda i, j: (i, 0, j, 0, 0))],
        out_shape=[jax.ShapeDtypeStruct((b, hh, MLA_QK, s), jnp.bfloat16),
                   jax.ShapeDtypeStruct((b, hh, s, MLA_QK), jnp.bfloat16),
                   jax.ShapeDtypeStruct((b, hh, s // tk, ATTN_VROWS, tk), jnp.bfloat16)],
        compiler_params=_cparams(("parallel", "parallel")),
        name="mla_up",
    )(lat3, cos, sin, qg, kvg, wqn, wqr, wqrot, wkn, wv)


def _attn_kernel(qt_ref, k_ref, vt_ref, o_ref, m_ref, acc_ref, *, unroll):
    n_chunks, _, tk = vt_ref.shape
    n_sub = qt_ref.shape[1] // ATTN_QSUB
    m_ref[...] = jnp.full_like(m_ref, -jnp.inf)
    acc_ref[...] = jnp.zeros_like(acc_ref)

    def scores(c, j):
        rows = pl.ds(pl.multiple_of(c * tk, tk), tk)
        return _bdot(k_ref[rows, :], qt_ref[:, j * ATTN_QSUB:(j + 1) * ATTN_QSUB])

    def group(g, carry):
        chains = [(g * unroll + u, j) for u in range(unroll) for j in range(n_sub)]
        pending = [scores(*ch) for ch in chains[:ATTN_LOOKAHEAD]]
        for idx, (c, j) in enumerate(chains):
            cols = slice(j * ATTN_QSUB, (j + 1) * ATTN_QSUB)
            s = pending.pop(0)
            if idx + ATTN_LOOKAHEAD < len(chains):
                pending.append(scores(*chains[idx + ATTN_LOOKAHEAD]))
            m_old = m_ref[:, cols]
            m_new = jnp.maximum(m_old, jnp.max(s, axis=0, keepdims=True))
            alpha = jnp.exp2(m_old - m_new)
            p = jnp.exp2(s - m_new)
            acc_ref[:, cols] = alpha * acc_ref[:, cols] + _bdot(vt_ref[c], p.astype(jnp.bfloat16))
            m_ref[:, cols] = m_new
        return carry

    n_groups = n_chunks // unroll
    if n_groups == 1:
        group(0, 0)
    else:
        lax.fori_loop(0, n_groups, group, 0)
    o_ref[...] = (acc_ref[:MLA_V, :] / acc_ref[MLA_V:MLA_V + 1, :]).T.astype(o_ref.dtype)


def _attn(qt, kf, vt, tq, unroll):
    b, hh, _, s = qt.shape
    n_chunks, _, tk = vt.shape[2:]
    return pl.pallas_call(
        functools.partial(_attn_kernel, unroll=unroll),
        grid=(b, hh, s // tq),
        in_specs=[pl.BlockSpec((None, None, MLA_QK, tq), lambda i, h, q: (i, h, 0, q)),
                  pl.BlockSpec((None, None, s, MLA_QK), lambda i, h, q: (i, h, 0, 0)),
                  pl.BlockSpec((None, None, n_chunks, ATTN_VROWS, tk),
                               lambda i, h, q: (i, h, 0, 0, 0))],
        out_specs=pl.BlockSpec((None, tq, MLA_V), lambda i, h, q: (i, q, h)),
        out_shape=jax.ShapeDtypeStruct((b, s, MLA_V_W), jnp.bfloat16),
        scratch_shapes=[pltpu.VMEM((1, tq), jnp.float32),
                        pltpu.VMEM((ATTN_VROWS, tq), jnp.float32)],
        compiler_params=_cparams(("parallel", "parallel", "parallel")),
        name="attn",
    )(qt, kf, vt)


def _split_bf16(x, pieces):
    out = []
    for _ in range(pieces):
        term = x.astype(jnp.bfloat16)
        out.append(term)
        x = x - term.astype(jnp.float32)
    return out


def _gla_cumsum(g, reverse, pieces):
    c = GLA_KERNEL_CHUNK
    row = lax.broadcasted_iota(jnp.int32, (c, c), 0)
    col = lax.broadcasted_iota(jnp.int32, (c, c), 1)
    tri = ((col >= row) if reverse else (col <= row)).astype(jnp.bfloat16)
    return _bdot(jnp.concatenate([tri] * pieces, axis=1),
                 jnp.concatenate(_split_bf16(g, pieces), axis=0))


def _gla_scores_factorised(q, k, b, keep):
    c = GLA_KERNEL_CHUNK
    d = b - b[c // 2:c // 2 + 1, :]
    q_in = (q * jnp.exp2(jnp.minimum(d, GLA_EXP2_CLAMP))).astype(jnp.bfloat16)
    k_in = (k * jnp.exp2(jnp.minimum(-d, GLA_EXP2_CLAMP))).astype(jnp.bfloat16)
    zeros = jnp.zeros((c, GLA_DK), jnp.bfloat16)
    keep2 = jnp.concatenate([keep, keep], axis=1)
    scores = []
    for hp in range(GLA_HEADS // 2):
        k0 = k_in[:, (2 * hp) * GLA_DK:(2 * hp + 1) * GLA_DK]
        k1 = k_in[:, (2 * hp + 1) * GLA_DK:(2 * hp + 2) * GLA_DK]
        k_diag = jnp.concatenate([jnp.concatenate([k0, zeros], axis=1),
                                  jnp.concatenate([zeros, k1], axis=1)], axis=0)
        a2 = _bdot_nt(q_in[:, 2 * hp * GLA_DK:(2 * hp + 2) * GLA_DK], k_diag)
        a2 = jnp.where(keep2, a2, 0.0).astype(jnp.bfloat16)
        scores += [a2[:, :c], a2[:, c:]]
    return scores


def _gla_scores_direct(q, k, b, keep, b_scr, k_scr):
    c = GLA_KERNEL_CHUNK
    b_scr[...] = b
    k_scr[...] = k
    col = lax.broadcasted_iota(jnp.int32, (c, c), 1)

    def key_row(s, accs):
        e = jnp.exp2(jnp.minimum(b - b_scr[pl.ds(s, 1), :], 0.0)) * (q * k_scr[pl.ds(s, 1), :])
        return tuple(
            jnp.where(col == s,
                      jnp.sum(e[:, h * GLA_DK:(h + 1) * GLA_DK], axis=1, keepdims=True), acc)
            for h, acc in enumerate(accs))

    accs = lax.fori_loop(0, c, key_row,
                         tuple(jnp.zeros((c, c), jnp.float32) for _ in range(GLA_HEADS)))
    return [jnp.where(keep, a, 0.0).astype(jnp.bfloat16) for a in accs]


def _gla_chunk_local(qkv_ref, b, rows, reverse, direct_scratch):
    c = GLA_KERNEL_CHUNK
    row = lax.broadcasted_iota(jnp.int32, (c, c), 0)
    col = lax.broadcasted_iota(jnp.int32, (c, c), 1)
    keep = (col > row) if reverse else (col <= row)
    last = 0 if reverse else c - 1
    b_last = b[last:last + 1, :]
    q = qkv_ref[rows, 0:GLA_QK_W].astype(jnp.float32)
    k = qkv_ref[rows, GLA_QK_W:2 * GLA_QK_W].astype(jnp.float32)
    q_st = (q * jnp.exp2(b)).astype(jnp.bfloat16)
    k_st = (k * jnp.exp2(b_last - b)).astype(jnp.bfloat16)
    e_last = jnp.exp2(b_last)
    if direct_scratch is None:
        scores = _gla_scores_factorised(q, k, b, keep)
    else:
        scores = _gla_scores_direct(q, k, b, keep, *direct_scratch)
    heads = []
    for h in range(GLA_HEADS):
        ks = slice(h * GLA_DK, (h + 1) * GLA_DK)
        v = qkv_ref[rows, 2 * GLA_QK_W + h * GLA_DV:2 * GLA_QK_W + (h + 1) * GLA_DV]
        upd = _bdot_tn(k_st[:, ks], v)
        decay_col = jnp.broadcast_to(e_last[:, ks], (GLA_DK, GLA_DK)).T
        decay = jnp.concatenate([decay_col] * (GLA_DV // GLA_DK), axis=1)
        heads.append((jnp.concatenate([scores[h], q_st[:, ks]], axis=1), v, upd, decay))
    return heads


def _gla_chunk_state(heads, rows, o_ref, state_ref):
    for h, (lhs, v, upd, decay) in enumerate(heads):
        st = state_ref[h]
        o = _bdot(lhs, jnp.concatenate([v, st.astype(jnp.bfloat16)], axis=0))
        o_ref[rows, h * GLA_DV:(h + 1) * GLA_DV] = o.astype(o_ref.dtype)
        state_ref[h] = st * decay + upd


def _gla_block(qkv_f_ref, g_f_ref, qkv_b_ref, g_b_ref, of_ref, ob_ref, sf_ref, sb_ref,
               direct_scratch):
    c = GLA_KERNEL_CHUNK
    n_chunks = qkv_f_ref.shape[0] // c
    work = []
    for ci in range(n_chunks):
        rows_f = slice(ci * c, (ci + 1) * c)
        rows_b = slice((n_chunks - 1 - ci) * c, (n_chunks - ci) * c)
        work.append((qkv_f_ref, g_f_ref, rows_f, False, of_ref, sf_ref))
        work.append((qkv_b_ref, g_b_ref, rows_b, True, ob_ref, sb_ref))
    pieces = 2 if direct_scratch is None else 3
    sums = [_gla_cumsum(g_ref[rows, :], rev, pieces) for (_, g_ref, rows, rev, _, _) in work]
    local = lambda i: _gla_chunk_local(work[i][0], sums[i], work[i][2], work[i][3], direct_scratch)
    pending = [local(i) for i in range(min(GLA_LOOKAHEAD, len(work)))]
    for i, (_, _, rows, _, o_ref, state_ref) in enumerate(work):
        heads = pending.pop(0)
        if i + GLA_LOOKAHEAD < len(work):
            pending.append(local(i + GLA_LOOKAHEAD))
        _gla_chunk_state(heads, rows, o_ref, state_ref)


def _gla_kernel(qkv_f_ref, g_f_ref, qkv_b_ref, g_b_ref, of_ref, ob_ref, sf_ref, sb_ref,
                b_scr, k_scr):
    @pl.when(pl.program_id(1) == 0)
    def _():
        sf_ref[...] = jnp.zeros_like(sf_ref)
        sb_ref[...] = jnp.zeros_like(sb_ref)

    refs = (qkv_f_ref, g_f_ref, qkv_b_ref, g_b_ref, of_ref, ob_ref, sf_ref, sb_ref)
    g_min = jnp.minimum(jnp.min(g_f_ref[...]), jnp.min(g_b_ref[...]))
    steep = g_min * (GLA_KERNEL_CHUNK // 2) < -GLA_EXP2_CLAMP
    lax.cond(steep,
             lambda: _gla_block(*refs, (b_scr, k_scr)),
             lambda: _gla_block(*refs, None))


def _gla(qkv3, gates3, tb):
    b, s, _ = qkv3.shape
    nb = s // tb
    fwd = lambda i, j: (i, j, 0)
    bwd = lambda i, j: (i, nb - 1 - j, 0)
    return pl.pallas_call(
        _gla_kernel,
        grid=(b, nb),
        in_specs=[pl.BlockSpec((None, tb, QKV_W), fwd),
                  pl.BlockSpec((None, tb, GLA_QK_W), fwd),
                  pl.BlockSpec((None, tb, QKV_W), bwd),
                  pl.BlockSpec((None, tb, GLA_QK_W), lambda i, j: (i, nb - 1 - j, 1))],
        out_specs=[pl.BlockSpec((None, tb, GLA_V_W), fwd),
                   pl.BlockSpec((None, tb, GLA_V_W), bwd)],
        out_shape=[jax.ShapeDtypeStruct((b, s, GLA_V_W), jnp.bfloat16),
                   jax.ShapeDtypeStruct((b, s, GLA_V_W), jnp.bfloat16)],
        scratch_shapes=[pltpu.VMEM((GLA_HEADS, GLA_DK, GLA_DV), jnp.float32),
                        pltpu.VMEM((GLA_HEADS, GLA_DK, GLA_DV), jnp.float32),
                        pltpu.VMEM((GLA_KERNEL_CHUNK, GLA_QK_W), jnp.float32),
                        pltpu.VMEM((GLA_KERNEL_CHUNK, GLA_QK_W), jnp.float32)],
        compiler_params=_cparams(("parallel", "arbitrary")),
        name="gla",
    )(qkv3, gates3, qkv3, gates3)


def _final_kernel(x_ref, of_ref, ob_ref, om_ref, act_ref, gg_ref, pg_ref, wa_ref, wb_ref, wo_ref,
                  y_ref):
    o_sum = of_ref[...].astype(jnp.float32) + ob_ref[...].astype(jnp.float32)
    gg = gg_ref[...]
    o_a = jnp.concatenate(
        [_rms(o_sum[:, h * GLA_DV:(h + 1) * GLA_DV], gg) for h in range(GLA_HEADS)], axis=1)
    y_a = (o_a * act_ref[:, 0:D_MODEL].astype(jnp.float32)).astype(jnp.bfloat16)
    y_b = om_ref[...] * act_ref[:, D_MODEL:2 * D_MODEL]
    merged = (act_ref[:, 2 * D_MODEL:3 * D_MODEL].astype(jnp.float32) * _bdot(y_a, wa_ref[...])
              + act_ref[:, 3 * D_MODEL:].astype(jnp.float32) * _bdot(y_b, wb_ref[...]))
    out = _bdot(merged.astype(jnp.bfloat16), wo_ref[...])
    y_ref[...] = x_ref[...] + _rms(out, pg_ref[...])


def _final(x2d, o_f, o_b, o_m, act, gg, pg, wa, wb, wo, ts):
    t = x2d.shape[0]
    tok = lambda w: pl.BlockSpec((ts, w), lambda i: (i, 0))
    return pl.pallas_call(
        _final_kernel,
        grid=(t // ts,),
        in_specs=[tok(D_MODEL), tok(GLA_V_W), tok(GLA_V_W), tok(MLA_V_W), tok(ACT_W),
                  _const_spec((1, GLA_DV)), _const_spec((1, D_MODEL)),
                  _const_spec((GLA_V_W, D_MODEL)), _const_spec((MLA_V_W, D_MODEL)),
                  _const_spec((D_MODEL, D_MODEL))],
        out_specs=tok(D_MODEL),
        out_shape=jax.ShapeDtypeStruct((t, D_MODEL), jnp.float32),
        compiler_params=_cparams(("parallel",)),
        name="final",
    )(x2d, o_f, o_b, o_m, act, gg, pg, wa, wb, wo)


def _rotate_half_cols(w):
    half = MLA_ROPE // 2
    w3 = w.reshape(w.shape[0], -1, MLA_ROPE)
    return jnp.concatenate([-w3[..., half:], w3[..., :half]], axis=-1).reshape(w.shape)


def _pad_cols(w, width):
    return jnp.pad(w, ((0, 0), (0, width - w.shape[1])))


def _prepare_weights(pre_g, w_in, wg2_f, bg2_f, wg2_b, bg2_b, gla_g, q_g, w_uq, kv_g, w_ukv,
                     w_a, w_b, w_o, post_g):
    bf = jnp.bfloat16
    sizes = (GLA_QK_W, GLA_QK_W, GLA_V_W, GLA_GATE_RANK, GLA_GATE_RANK, GLA_V_W,
             MLA_Q_RANK, MLA_KV_RANK, MLA_ROPE, MLA_V_W, D_MODEL, D_MODEL)
    offs = [0]
    for sz in sizes:
        offs.append(offs[-1] + sz)
    (w_q, w_k, w_v, w_grf, w_grb, w_za, w_cq, w_ckv, w_kr, w_zb, w_ga, w_gb) = [
        w_in[:, offs[i]:offs[i + 1]] for i in range(len(sizes))]
    wqkv = jnp.concatenate([w_q, w_k, w_v], axis=1).astype(bf)
    wlat = jnp.concatenate([
        w_cq, w_ckv, _pad_cols(w_kr, V7X_LANES), _pad_cols(_rotate_half_cols(w_kr), V7X_LANES),
        _pad_cols(jnp.concatenate([w_grf, w_grb], axis=1), V7X_LANES)], axis=1).astype(bf)
    wact = jnp.concatenate([w_za, w_zb, w_ga, w_gb], axis=1).astype(bf)

    wg = jnp.zeros((V7X_LANES, GATES_W), jnp.float32)
    wg = wg.at[0:GLA_GATE_RANK, :GLA_QK_W].set(wg2_f)
    wg = wg.at[GLA_GATE_RANK:2 * GLA_GATE_RANK, GLA_QK_W:].set(wg2_b).astype(bf)
    bg = jnp.concatenate([bg2_f, bg2_b])[None, :]

    uq = w_uq.reshape(MLA_Q_RANK, MLA_HEADS, MLA_QK)
    wqn = uq[:, :, :MLA_NOPE].reshape(MLA_Q_RANK, MLA_HEADS * MLA_NOPE)
    wqr = uq[:, :, MLA_NOPE:].reshape(MLA_Q_RANK, MLA_HEADS * MLA_ROPE)
    ukv = w_ukv.reshape(MLA_KV_RANK, MLA_HEADS, MLA_NOPE + MLA_V)
    wkn = ukv[:, :, :MLA_NOPE].reshape(MLA_KV_RANK, MLA_HEADS * MLA_NOPE)
    wv = ukv[:, :, MLA_NOPE:].reshape(MLA_KV_RANK, MLA_HEADS * MLA_V)
    return dict(
        pre_g=pre_g[None, :], wqkv=wqkv, wlat=wlat, wact=wact,
        wg=wg, bg=bg,
        gla_g=gla_g[None, :], q_g=q_g[None, :], kv_g=kv_g[None, :],
        wqn=wqn.astype(bf), wqr=wqr.astype(bf), wqrot=_rotate_half_cols(wqr).astype(bf),
        wkn=wkn.astype(bf), wv=wv.astype(bf),
        w_a=w_a.astype(bf), w_b=w_b.astype(bf), w_o=w_o.astype(bf), post_g=post_g[None, :])


def _rope_tables(s):
    half = MLA_ROPE // 2
    freqs = ROPE_THETA ** (-jnp.arange(half, dtype=jnp.float32) / half)
    ang = jnp.arange(s).astype(jnp.float32)[:, None] * freqs[None, :]
    cos = jnp.cos(ang)
    sin = jnp.sin(ang)
    return jnp.concatenate([cos, cos], axis=1), jnp.concatenate([sin, sin], axis=1)


def _tiles(b, s):
    t = b * s
    tk = min(512, s)
    return dict(ts=min(512, t), ts_up=min(1024, s), tk=tk, tb=min(512, s), tq=min(2048, s),
                attn_unroll=min(8, s // tk))


def _layer(x, w):
    b, s, _ = x.shape
    t = b * s
    tl = _tiles(b, s)
    x2d = x.reshape(t, D_MODEL)
    qkv, lat, act, gates = _proj(x2d, w["pre_g"], w["wqkv"], w["wlat"], w["wact"], w["wg"], w["bg"],
                                 tl["ts"])
    lat3 = lat.reshape(b, s, LAT_GR)
    cos, sin = _rope_tables(s)
    qt, kf, vt = _mla_up(lat3, cos, sin, w["q_g"], w["kv_g"], w["wqn"], w["wqr"], w["wqrot"],
                         w["wkn"], w["wv"], tl["ts_up"], tl["tk"])
    o_m = _attn(qt, kf, vt, tl["tq"], tl["attn_unroll"])
    o_f, o_b = _gla(qkv.reshape(b, s, QKV_W), gates.reshape(b, s, GATES_W), tl["tb"])
    y = _final(x2d, o_f.reshape(t, GLA_V_W), o_b.reshape(t, GLA_V_W), o_m.reshape(t, MLA_V_W),
               act, w["gla_g"], w["post_g"], w["w_a"], w["w_b"], w["w_o"], tl["ts"])
    return y.reshape(b, s, D_MODEL)


def kernel(x_prompt, x_sample, pre_norm_g, w_in, gla_wg2_fwd, gla_bg2_fwd, gla_wg2_bwd, gla_bg2_bwd,
           gla_out_norm_g, mla_q_norm_g, mla_w_uq, mla_kv_norm_g, mla_w_ukv, w_branch_a, w_branch_b,
           w_out, post_norm_g):
    depth = pre_norm_g.shape[0]
    y_prompt, y_sample = x_prompt, x_sample
    for l in range(depth):
        w = _prepare_weights(pre_norm_g[l], w_in[l], gla_wg2_fwd[l], gla_bg2_fwd[l], gla_wg2_bwd[l],
                             gla_bg2_bwd[l], gla_out_norm_g[l], mla_q_norm_g[l], mla_w_uq[l],
                             mla_kv_norm_g[l], mla_w_ukv[l], w_branch_a[l], w_branch_b[l],
                             w_out[l], post_norm_g[l])
        y_prompt = _layer(y_prompt, w)
        y_sample = _layer(y_sample, w)
    return (y_prompt, y_sample)
```

```python
import functools
import math

import jax
import jax.numpy as jnp
from jax import lax
from jax.experimental import pallas as pl
from jax.experimental.pallas import tpu as pltpu

D_MODEL = 1024
GLA_HEADS = 4
GLA_DK = 128
GLA_DV = 256
GLA_GATE_RANK = 16
GLA_GATE_NORM = 16.0
MLA_HEADS = 8
MLA_Q_RANK = 256
MLA_KV_RANK = 128
MLA_NOPE = 128
MLA_ROPE = 64
MLA_V = 128
ROPE_THETA = 10000.0
EPS = 1e-6

GLA_QK_W = GLA_HEADS * GLA_DK
GLA_V_W = GLA_HEADS * GLA_DV
MLA_V_W = MLA_HEADS * MLA_V
MLA_QK = MLA_NOPE + MLA_ROPE

V7X_LANES = 128
V7X_VMEM_LIMIT_BYTES = 56 * 1024 * 1024

QKV_W = 2 * GLA_QK_W + GLA_V_W
LAT_CQ = 0
LAT_CKV = MLA_Q_RANK
LAT_KR = LAT_CKV + MLA_KV_RANK
LAT_KROT = LAT_KR + V7X_LANES
LAT_GR = LAT_KROT + V7X_LANES
LAT_W = LAT_GR + V7X_LANES
ACT_W = 4 * D_MODEL
GATES_W = 2 * GLA_QK_W

GLA_KERNEL_CHUNK = 128
GLA_LOOKAHEAD = 4
GLA_EXP2_CLAMP = 115.0
LOG2E = 1.4426950408889634
ATTN_QSUB = 256
ATTN_LOOKAHEAD = 3
ATTN_BOUND_SLACK = 1.02
ATTN_MAX_BOUND = 50.0
ATTN_VROWS = MLA_V + 16


def _cparams(semantics):
    return pltpu.CompilerParams(dimension_semantics=semantics,
                                vmem_limit_bytes=V7X_VMEM_LIMIT_BYTES)


def _const_spec(shape):
    return pl.BlockSpec(shape, lambda *_: (0,) * len(shape))


def _rms(x, g):
    return x * lax.rsqrt(jnp.mean(x * x, axis=-1, keepdims=True) + EPS) * g


def _sigmoid(x):
    return 1.0 / (1.0 + jnp.exp(-x))


def _bdot(a, b):
    return jnp.dot(a, b, preferred_element_type=jnp.float32)


def _bdot_nt(a, b):
    return lax.dot_general(a, b, (((1,), (1,)), ((), ())), preferred_element_type=jnp.float32)


def _bdot_tn(a, b):
    return lax.dot_general(a, b, (((0,), (0,)), ((), ())), preferred_element_type=jnp.float32)


def _proj_kernel(x_ref, g_ref, wqkv_ref, wlat_ref, wact_ref, wg_ref, bg_ref,
                 qkv_ref, lat_ref, act_ref, gates_ref):
    half = x_ref.shape[0] // 2
    for r in (slice(0, half), slice(half, 2 * half)):
        h = _rms(x_ref[r, :], g_ref[...]).astype(jnp.bfloat16)
        lat = _bdot(h, wlat_ref[...])
        xg = _bdot(lat[:, LAT_GR:].astype(jnp.bfloat16), wg_ref[...]) + bg_ref[...]
        qkv = _bdot(h, wqkv_ref[...])
        lat_ref[r, :] = lat[:, :LAT_GR]
        gates_ref[r, :] = ((jnp.minimum(xg, 0.0) - jnp.log(1.0 + jnp.exp(-jnp.abs(xg))))
                           * (LOG2E / GLA_GATE_NORM))
        z = _bdot(h, wact_ref[:, :2 * D_MODEL])
        qkv_ref[r, :GLA_QK_W] = (qkv[:, :GLA_QK_W] * (GLA_DK ** -0.5)).astype(jnp.bfloat16)
        qkv_ref[r, GLA_QK_W:] = qkv[:, GLA_QK_W:].astype(jnp.bfloat16)
        gate = _bdot(h, wact_ref[:, 2 * D_MODEL:])
        act_ref[r, :2 * D_MODEL] = (z * _sigmoid(z)).astype(jnp.bfloat16)
        act_ref[r, 2 * D_MODEL:] = _sigmoid(gate).astype(jnp.bfloat16)


def _proj(x2d, pre_g, wqkv, wlat, wact, wg, bg, ts):
    t = x2d.shape[0]
    return pl.pallas_call(
        _proj_kernel,
        grid=(t // ts,),
        in_specs=[pl.BlockSpec((ts, D_MODEL), lambda i: (i, 0)),
                  _const_spec((1, D_MODEL)),
                  _const_spec((D_MODEL, QKV_W)),
                  _const_spec((D_MODEL, LAT_W)),
                  _const_spec((D_MODEL, ACT_W)),
                  _const_spec((V7X_LANES, GATES_W)),
                  _const_spec((1, GATES_W))],
        out_specs=[pl.BlockSpec((ts, QKV_W), lambda i: (i, 0)),
                   pl.BlockSpec((ts, LAT_GR), lambda i: (i, 0)),
                   pl.BlockSpec((ts, ACT_W), lambda i: (i, 0)),
                   pl.BlockSpec((ts, GATES_W), lambda i: (i, 0))],
        out_shape=[jax.ShapeDtypeStruct((t, QKV_W), jnp.bfloat16),
                   jax.ShapeDtypeStruct((t, LAT_GR), jnp.float32),
                   jax.ShapeDtypeStruct((t, ACT_W), jnp.bfloat16),
                   jax.ShapeDtypeStruct((t, GATES_W), jnp.float32)],
        compiler_params=_cparams(("parallel",)),
        name="proj",
    )(x2d, pre_g, wqkv, wlat, wact, wg, bg)


def _mla_up_kernel(lat_ref, cos_ref, sin_ref, qg_ref, kvg_ref, wqn_ref, wqr_ref, wqrot_ref,
                   wkn_ref, wv_ref, qt_ref, kf_ref, vt_ref, qn2_ref, kn2_ref):
    lat = lat_ref[...]
    cos = cos_ref[...]
    sin = sin_ref[...]
    c_q = _rms(lat[:, LAT_CQ:LAT_CQ + MLA_Q_RANK], qg_ref[...]).astype(jnp.bfloat16)
    c_kv = _rms(lat[:, LAT_CKV:LAT_CKV + MLA_KV_RANK], kvg_ref[...]).astype(jnp.bfloat16)

    q_scale = (MLA_QK ** -0.5) * LOG2E
    qn_t = (_bdot(c_q, wqn_ref[...]) * q_scale).T
    cos_h = jnp.concatenate([cos, cos], axis=1)
    sin_h = jnp.concatenate([sin, sin], axis=1)
    cos_q = jnp.concatenate([cos_h] * (MLA_HEADS // 2), axis=1)
    sin_q = jnp.concatenate([sin_h] * (MLA_HEADS // 2), axis=1)
    q_pe = _bdot(c_q, wqr_ref[...]) * cos_q + _bdot(c_q, wqrot_ref[...]) * sin_q
    qpe_t = (q_pe * q_scale).T

    k_pe = (lat[:, LAT_KR:LAT_KR + MLA_ROPE] * cos
            + lat[:, LAT_KROT:LAT_KROT + MLA_ROPE] * sin).astype(jnp.bfloat16)
    kn = _bdot(c_kv, wkn_ref[...]).astype(jnp.bfloat16)
    v_t = _bdot(c_kv, wv_ref[...]).T

    n_kc, tk = vt_ref.shape[1], vt_ref.shape[3]
    tile_row = lax.broadcasted_iota(jnp.int32, (ATTN_VROWS - MLA_V, tk), 0)
    ones_tile = (tile_row == 0).astype(jnp.bfloat16)
    ones_rows = jnp.ones((8, MLA_QK), jnp.bfloat16)
    kpe_sq = k_pe.astype(jnp.float32) ** 2
    for h in range(MLA_HEADS):
        q_nope = qn_t[h * MLA_NOPE:(h + 1) * MLA_NOPE, :].astype(jnp.bfloat16)
        q_rope = qpe_t[h * MLA_ROPE:(h + 1) * MLA_ROPE, :].astype(jnp.bfloat16)
        k_nope = kn[:, h * MLA_NOPE:(h + 1) * MLA_NOPE]
        qt_ref[h, :MLA_NOPE, :] = q_nope
        qt_ref[h, MLA_NOPE:, :] = q_rope
        kf_ref[h, :, :MLA_NOPE] = k_nope
        kf_ref[h, :, MLA_NOPE:] = k_pe
        qn2_ref[h] = (jnp.sum(q_nope.astype(jnp.float32) ** 2, axis=0, keepdims=True)
                      + jnp.sum(q_rope.astype(jnp.float32) ** 2, axis=0, keepdims=True))
        k_sq = jnp.concatenate([k_nope.astype(jnp.float32) ** 2, kpe_sq], axis=1)
        kn2_ref[h] = _bdot_nt(ones_rows, k_sq.astype(jnp.bfloat16))[0:1, :]
        for u in range(n_kc):
            vt_ref[h, u, :MLA_V, :] = v_t[h * MLA_V:(h + 1) * MLA_V,
                                          u * tk:(u + 1) * tk].astype(jnp.bfloat16)
            vt_ref[h, u, MLA_V:, :] = ones_tile


def _mla_up(lat3, cos, sin, qg, kvg, wqn, wqr, wqrot, wkn, wv, ts, tk):
    b, s, _ = lat3.shape
    hh = MLA_HEADS
    return pl.pallas_call(
        _mla_up_kernel,
        grid=(b, s // ts),
        in_specs=[pl.BlockSpec((None, ts, LAT_GR), lambda i, j: (i, j, 0)),
                  pl.BlockSpec((ts, MLA_ROPE), lambda i, j: (j, 0)),
                  pl.BlockSpec((ts, MLA_ROPE), lambda i, j: (j, 0)),
                  _const_spec((1, MLA_Q_RANK)),
                  _const_spec((1, MLA_KV_RANK)),
                  _const_spec((MLA_Q_RANK, hh * MLA_NOPE)),
                  _const_spec((MLA_Q_RANK, hh * MLA_ROPE)),
                  _const_spec((MLA_Q_RANK, hh * MLA_ROPE)),
                  _const_spec((MLA_KV_RANK, hh * MLA_NOPE)),
                  _const_spec((MLA_KV_RANK, hh * MLA_V))],
        out_specs=[pl.BlockSpec((None, hh, MLA_QK, ts), lambda i, j: (i, 0, 0, j)),
                   pl.BlockSpec((None, hh, ts, MLA_QK), lambda i, j: (i, 0, j, 0)),
                   pl.BlockSpec((None, hh, ts // tk, ATTN_VROWS, tk),
                                lambda i, j: (i, 0, j, 0, 0)),
                   pl.BlockSpec((None, hh, 1, ts), lambda i, j: (i, 0, 0, j)),
                   pl.BlockSpec((None, hh, 1, ts), lambda i, j: (i, 0, 0, j))],
        out_shape=[jax.ShapeDtypeStruct((b, hh, MLA_QK, s), jnp.bfloat16),
                   jax.ShapeDtypeStruct((b, hh, s, MLA_QK), jnp.bfloat16),
                   jax.ShapeDtypeStruct((b, hh, s // tk, ATTN_VROWS, tk), jnp.bfloat16),
                   jax.ShapeDtypeStruct((b, hh, 1, s), jnp.float32),
                   jax.ShapeDtypeStruct((b, hh, 1, s), jnp.float32)],
        compiler_params=_cparams(("parallel", "parallel")),
        name="mla_up",
    )(lat3, cos, sin, qg, kvg, wqn, wqr, wqrot, wkn, wv)


def _attn_kernel(qt_ref, k_ref, vt_ref, qn2_ref, kn2_ref, o_ref, m_ref, acc_ref, *, unroll):
    n_chunks, _, tk = vt_ref.shape
    n_sub = qt_ref.shape[1] // ATTN_QSUB

    def scores(c, j):
        rows = pl.ds(pl.multiple_of(c * tk, tk), tk)
        return _bdot(k_ref[rows, :], qt_ref[:, j * ATTN_QSUB:(j + 1) * ATTN_QSUB])

    def run(chain_tail):
        acc_ref[...] = jnp.zeros_like(acc_ref)

        def group(g, carry):
            chains = [(g * unroll + u, j) for u in range(unroll) for j in range(n_sub)]
            pending = [scores(*ch) for ch in chains[:ATTN_LOOKAHEAD]]
            for idx, (c, j) in enumerate(chains):
                s = pending.pop(0)
                if idx + ATTN_LOOKAHEAD < len(chains):
                    pending.append(scores(*chains[idx + ATTN_LOOKAHEAD]))
                chain_tail(s, c, slice(j * ATTN_QSUB, (j + 1) * ATTN_QSUB))
            return carry

        n_groups = n_chunks // unroll
        if n_groups == 1:
            group(0, 0)
        else:
            lax.fori_loop(0, n_groups, group, 0)
        o_ref[...] = (acc_ref[:MLA_V, :] / acc_ref[MLA_V:MLA_V + 1, :]).T.astype(o_ref.dtype)

    k_max2 = jnp.max(kn2_ref[...], axis=1, keepdims=True)
    bound = jnp.sqrt(qn2_ref[...] * k_max2) * ATTN_BOUND_SLACK

    def bounded_tail(s, c, cols):
        p = jnp.exp2(s - bound[:, cols])
        acc_ref[:, cols] = acc_ref[:, cols] + _bdot(vt_ref[c], p.astype(jnp.bfloat16))

    def online_tail(s, c, cols):
        m_old = m_ref[:, cols]
        m_new = jnp.maximum(m_old, jnp.max(s, axis=0, keepdims=True))
        alpha = jnp.exp2(m_old - m_new)
        p = jnp.exp2(s - m_new)
        acc_ref[:, cols] = alpha * acc_ref[:, cols] + _bdot(vt_ref[c], p.astype(jnp.bfloat16))
        m_ref[:, cols] = m_new

    def online():
        m_ref[...] = jnp.full_like(m_ref, -jnp.inf)
        run(online_tail)

    lax.cond(jnp.max(bound) > ATTN_MAX_BOUND, online, lambda: run(bounded_tail))


def _attn(qt, kf, vt, qn2, kn2, tq, unroll):
    b, hh, _, s = qt.shape
    n_chunks, _, tk = vt.shape[2:]
    return pl.pallas_call(
        functools.partial(_attn_kernel, unroll=unroll),
        grid=(b, hh, s // tq),
        in_specs=[pl.BlockSpec((None, None, MLA_QK, tq), lambda i, h, q: (i, h, 0, q)),
                  pl.BlockSpec((None, None, s, MLA_QK), lambda i, h, q: (i, h, 0, 0)),
                  pl.BlockSpec((None, None, n_chunks, ATTN_VROWS, tk),
                               lambda i, h, q: (i, h, 0, 0, 0)),
                  pl.BlockSpec((None, None, 1, tq), lambda i, h, q: (i, h, 0, q)),
                  pl.BlockSpec((None, None, 1, s), lambda i, h, q: (i, h, 0, 0))],
        out_specs=pl.BlockSpec((None, tq, MLA_V), lambda i, h, q: (i, q, h)),
        out_shape=jax.ShapeDtypeStruct((b, s, MLA_V_W), jnp.bfloat16),
        scratch_shapes=[pltpu.VMEM((1, tq), jnp.float32),
                        pltpu.VMEM((ATTN_VROWS, tq), jnp.float32)],
        compiler_params=_cparams(("parallel", "parallel", "parallel")),
        name="attn",
    )(qt, kf, vt, qn2, kn2)


def _split_bf16(x, pieces):
    out = []
    for _ in range(pieces):
        term = x.astype(jnp.bfloat16)
        out.append(term)
        x = x - term.astype(jnp.float32)
    return out


def _gla_cumsum(g, reverse, pieces):
    c = GLA_KERNEL_CHUNK
    row = lax.broadcasted_iota(jnp.int32, (c, c), 0)
    col = lax.broadcasted_iota(jnp.int32, (c, c), 1)
    tri = ((col >= row) if reverse else (col <= row)).astype(jnp.bfloat16)
    return _bdot(jnp.concatenate([tri] * pieces, axis=1),
                 jnp.concatenate(_split_bf16(g, pieces), axis=0))


def _gla_scores_factorised(q, k, b, keep):
    c = GLA_KERNEL_CHUNK
    d = b - b[c // 2:c // 2 + 1, :]
    q_in = (q * jnp.exp2(jnp.minimum(d, GLA_EXP2_CLAMP))).astype(jnp.bfloat16)
    k_in = (k * jnp.exp2(jnp.minimum(-d, GLA_EXP2_CLAMP))).astype(jnp.bfloat16)
    zeros = jnp.zeros((c, GLA_DK), jnp.bfloat16)
    keep2 = jnp.concatenate([keep, keep], axis=1)
    scores = []
    for hp in range(GLA_HEADS // 2):
        k0 = k_in[:, (2 * hp) * GLA_DK:(2 * hp + 1) * GLA_DK]
        k1 = k_in[:, (2 * hp + 1) * GLA_DK:(2 * hp + 2) * GLA_DK]
        k_diag = jnp.concatenate([jnp.concatenate([k0, zeros], axis=1),
                                  jnp.concatenate([zeros, k1], axis=1)], axis=0)
        a2 = _bdot_nt(q_in[:, 2 * hp * GLA_DK:(2 * hp + 2) * GLA_DK], k_diag)
        a2 = jnp.where(keep2, a2, 0.0).astype(jnp.bfloat16)
        scores += [a2[:, :c], a2[:, c:]]
    return scores


def _gla_scores_direct(q, k, b, keep, b_scr, k_scr):
    c = GLA_KERNEL_CHUNK
    b_scr[...] = b
    k_scr[...] = k
    col = lax.broadcasted_iota(jnp.int32, (c, c), 1)

    def key_row(s, accs):
        e = jnp.exp2(jnp.minimum(b - b_scr[pl.ds(s, 1), :], 0.0)) * (q * k_scr[pl.ds(s, 1), :])
        return tuple(
            jnp.where(col == s,
                      jnp.sum(e[:, h * GLA_DK:(h + 1) * GLA_DK], axis=1, keepdims=True), acc)
            for h, acc in enumerate(accs))

    accs = lax.fori_loop(0, c, key_row,
                         tuple(jnp.zeros((c, c), jnp.float32) for _ in range(GLA_HEADS)))
    return [jnp.where(keep, a, 0.0).astype(jnp.bfloat16) for a in accs]


def _gla_chunk_local(qkv_ref, b, rows, reverse, direct_scratch):
    c = GLA_KERNEL_CHUNK
    row = lax.broadcasted_iota(jnp.int32, (c, c), 0)
    col = lax.broadcasted_iota(jnp.int32, (c, c), 1)
    keep = (col > row) if reverse else (col <= row)
    last = 0 if reverse else c - 1
    b_last = b[last:last + 1, :]
    q = qkv_ref[rows, 0:GLA_QK_W].astype(jnp.float32)
    k = qkv_ref[rows, GLA_QK_W:2 * GLA_QK_W].astype(jnp.float32)
    q_st = (q * jnp.exp2(b)).astype(jnp.bfloat16)
    k_st = (k * jnp.exp2(b_last - b)).astype(jnp.bfloat16)
    e_last = jnp.exp2(b_last)
    if direct_scratch is None:
        scores = _gla_scores_factorised(q, k, b, keep)
    else:
        scores = _gla_scores_direct(q, k, b, keep, *direct_scratch)
    heads = []
    for h in range(GLA_HEADS):
        ks = slice(h * GLA_DK, (h + 1) * GLA_DK)
        v = qkv_ref[rows, 2 * GLA_QK_W + h * GLA_DV:2 * GLA_QK_W + (h + 1) * GLA_DV]
        upd = _bdot_tn(k_st[:, ks], v)
        decay_col = jnp.broadcast_to(e_last[:, ks], (GLA_DK, GLA_DK)).T
        decay = jnp.concatenate([decay_col] * (GLA_DV // GLA_DK), axis=1)
        heads.append((jnp.concatenate([scores[h], q_st[:, ks]], axis=1), v, upd, decay))
    return heads


def _gla_chunk_state(heads, rows, o_ref, state_ref):
    for h, (lhs, v, upd, decay) in enumerate(heads):
        st = state_ref[h]
        o = _bdot(lhs, jnp.concatenate([v, st.astype(jnp.bfloat16)], axis=0))
        o_ref[rows, h * GLA_DV:(h + 1) * GLA_DV] = o.astype(o_ref.dtype)
        state_ref[h] = st * decay + upd


def _gla_block(qkv_f_ref, g_f_ref, qkv_b_ref, g_b_ref, of_ref, ob_ref, sf_ref, sb_ref,
               direct_scratch):
    c = GLA_KERNEL_CHUNK
    n_chunks = qkv_f_ref.shape[0] // c
    work = []
    for ci in range(n_chunks):
        rows_f = slice(ci * c, (ci + 1) * c)
        rows_b = slice((n_chunks - 1 - ci) * c, (n_chunks - ci) * c)
        work.append((qkv_f_ref, g_f_ref, rows_f, False, of_ref, sf_ref))
        work.append((qkv_b_ref, g_b_ref, rows_b, True, ob_ref, sb_ref))
    pieces = 2 if direct_scratch is None else 3
    sums = [_gla_cumsum(g_ref[rows, :], rev, pieces) for (_, g_ref, rows, rev, _, _) in work]
    local = lambda i: _gla_chunk_local(work[i][0], sums[i], work[i][2], work[i][3], direct_scratch)
    pending = [local(i) for i in range(min(GLA_LOOKAHEAD, len(work)))]
    for i, (_, _, rows, _, o_ref, state_ref) in enumerate(work):
        heads = pending.pop(0)
        if i + GLA_LOOKAHEAD < len(work):
            pending.append(local(i + GLA_LOOKAHEAD))
        _gla_chunk_state(heads, rows, o_ref, state_ref)


def _gla_kernel(qkv_f_ref, g_f_ref, qkv_b_ref, g_b_ref, of_ref, ob_ref, sf_ref, sb_ref,
                b_scr, k_scr):
    @pl.when(pl.program_id(1) == 0)
    def _():
        sf_ref[...] = jnp.zeros_like(sf_ref)
        sb_ref[...] = jnp.zeros_like(sb_ref)

    refs = (qkv_f_ref, g_f_ref, qkv_b_ref, g_b_ref, of_ref, ob_ref, sf_ref, sb_ref)
    g_min = jnp.minimum(jnp.min(g_f_ref[...]), jnp.min(g_b_ref[...]))
    steep = g_min * (GLA_KERNEL_CHUNK // 2) < -GLA_EXP2_CLAMP
    lax.cond(steep,
             lambda: _gla_block(*refs, (b_scr, k_scr)),
             lambda: _gla_block(*refs, None))


def _gla(qkv3, gates3, tb):
    b, s, _ = qkv3.shape
    nb = s // tb
    fwd = lambda i, j: (i, j, 0)
    bwd = lambda i, j: (i, nb - 1 - j, 0)
    return pl.pallas_call(
        _gla_kernel,
        grid=(b, nb),
        in_specs=[pl.BlockSpec((None, tb, QKV_W), fwd),
                  pl.BlockSpec((None, tb, GLA_QK_W), fwd),
                  pl.BlockSpec((None, tb, QKV_W), bwd),
                  pl.BlockSpec((None, tb, GLA_QK_W), lambda i, j: (i, nb - 1 - j, 1))],
        out_specs=[pl.BlockSpec((None, tb, GLA_V_W), fwd),
                   pl.BlockSpec((None, tb, GLA_V_W), bwd)],
        out_shape=[jax.ShapeDtypeStruct((b, s, GLA_V_W), jnp.bfloat16),
                   jax.ShapeDtypeStruct((b, s, GLA_V_W), jnp.bfloat16)],
        scratch_shapes=[pltpu.VMEM((GLA_HEADS, GLA_DK, GLA_DV), jnp.float32),
                        pltpu.VMEM((GLA_HEADS, GLA_DK, GLA_DV), jnp.float32),
                        pltpu.VMEM((GLA_KERNEL_CHUNK, GLA_QK_W), jnp.float32),
                        pltpu.VMEM((GLA_KERNEL_CHUNK, GLA_QK_W), jnp.float32)],
        compiler_params=_cparams(("parallel", "arbitrary")),
        name="gla",
    )(qkv3, gates3, qkv3, gates3)


def _final_kernel(x_ref, of_ref, ob_ref, om_ref, act_ref, gg_ref, pg_ref, wa_ref, wb_ref, wo_ref,
                  y_ref):
    o_sum = of_ref[...].astype(jnp.float32) + ob_ref[...].astype(jnp.float32)
    gg = gg_ref[...]
    o_a = jnp.concatenate(
        [_rms(o_sum[:, h * GLA_DV:(h + 1) * GLA_DV], gg) for h in range(GLA_HEADS)], axis=1)
    y_a = (o_a * act_ref[:, 0:D_MODEL].astype(jnp.float32)).astype(jnp.bfloat16)
    y_b = om_ref[...] * act_ref[:, D_MODEL:2 * D_MODEL]
    merged = (act_ref[:, 2 * D_MODEL:3 * D_MODEL].astype(jnp.float32) * _bdot(y_a, wa_ref[...])
              + act_ref[:, 3 * D_MODEL:].astype(jnp.float32) * _bdot(y_b, wb_ref[...]))
    out = _bdot(merged.astype(jnp.bfloat16), wo_ref[...])
    y_ref[...] = x_ref[...] + _rms(out, pg_ref[...])


def _final(x2d, o_f, o_b, o_m, act, gg, pg, wa, wb, wo, ts):
    t = x2d.shape[0]
    tok = lambda w: pl.BlockSpec((ts, w), lambda i: (i, 0))
    return pl.pallas_call(
        _final_kernel,
        grid=(t // ts,),
        in_specs=[tok(D_MODEL), tok(GLA_V_W), tok(GLA_V_W), tok(MLA_V_W), tok(ACT_W),
                  _const_spec((1, GLA_DV)), _const_spec((1, D_MODEL)),
                  _const_spec((GLA_V_W, D_MODEL)), _const_spec((MLA_V_W, D_MODEL)),
                  _const_spec((D_MODEL, D_MODEL))],
        out_specs=tok(D_MODEL),
        out_shape=jax.ShapeDtypeStruct((t, D_MODEL), jnp.float32),
        compiler_params=_cparams(("parallel",)),
        name="final",
    )(x2d, o_f, o_b, o_m, act, gg, pg, wa, wb, wo)


def _rotate_half_cols(w):
    half = MLA_ROPE // 2
    w3 = w.reshape(w.shape[0], -1, MLA_ROPE)
    return jnp.concatenate([-w3[..., half:], w3[..., :half]], axis=-1).reshape(w.shape)


def _pad_cols(w, width):
    return jnp.pad(w, ((0, 0), (0, width - w.shape[1])))


def _prepare_weights(pre_g, w_in, wg2_f, bg2_f, wg2_b, bg2_b, gla_g, q_g, w_uq, kv_g, w_ukv,
                     w_a, w_b, w_o, post_g):
    bf = jnp.bfloat16
    sizes = (GLA_QK_W, GLA_QK_W, GLA_V_W, GLA_GATE_RANK, GLA_GATE_RANK, GLA_V_W,
             MLA_Q_RANK, MLA_KV_RANK, MLA_ROPE, MLA_V_W, D_MODEL, D_MODEL)
    offs = [0]
    for sz in sizes:
        offs.append(offs[-1] + sz)
    (w_q, w_k, w_v, w_grf, w_grb, w_za, w_cq, w_ckv, w_kr, w_zb, w_ga, w_gb) = [
        w_in[:, offs[i]:offs[i + 1]] for i in range(len(sizes))]
    wqkv = jnp.concatenate([w_q, w_k, w_v], axis=1).astype(bf)
    wlat = jnp.concatenate([
        w_cq, w_ckv, _pad_cols(w_kr, V7X_LANES), _pad_cols(_rotate_half_cols(w_kr), V7X_LANES),
        _pad_cols(jnp.concatenate([w_grf, w_grb], axis=1), V7X_LANES)], axis=1).astype(bf)
    wact = jnp.concatenate([w_za, w_zb, w_ga, w_gb], axis=1).astype(bf)

    wg = jnp.zeros((V7X_LANES, GATES_W), jnp.float32)
    wg = wg.at[0:GLA_GATE_RANK, :GLA_QK_W].set(wg2_f)
    wg = wg.at[GLA_GATE_RANK:2 * GLA_GATE_RANK, GLA_QK_W:].set(wg2_b).astype(bf)
    bg = jnp.concatenate([bg2_f, bg2_b])[None, :]

    uq = w_uq.reshape(MLA_Q_RANK, MLA_HEADS, MLA_QK)
    wqn = uq[:, :, :MLA_NOPE].reshape(MLA_Q_RANK, MLA_HEADS * MLA_NOPE)
    wqr = uq[:, :, MLA_NOPE:].reshape(MLA_Q_RANK, MLA_HEADS * MLA_ROPE)
    ukv = w_ukv.reshape(MLA_KV_RANK, MLA_HEADS, MLA_NOPE + MLA_V)
    wkn = ukv[:, :, :MLA_NOPE].reshape(MLA_KV_RANK, MLA_HEADS * MLA_NOPE)
    wv = ukv[:, :, MLA_NOPE:].reshape(MLA_KV_RANK, MLA_HEADS * MLA_V)
    return dict(
        pre_g=pre_g[None, :], wqkv=wqkv, wlat=wlat, wact=wact,
        wg=wg, bg=bg,
        gla_g=gla_g[None, :], q_g=q_g[None, :], kv_g=kv_g[None, :],
        wqn=wqn.astype(bf), wqr=wqr.astype(bf), wqrot=_rotate_half_cols(wqr).astype(bf),
        wkn=wkn.astype(bf), wv=wv.astype(bf),
        w_a=w_a.astype(bf), w_b=w_b.astype(bf), w_o=w_o.astype(bf), post_g=post_g[None, :])


def _rope_tables(s):
    half = MLA_ROPE // 2
    freqs = ROPE_THETA ** (-jnp.arange(half, dtype=jnp.float32) / half)
    ang = jnp.arange(s).astype(jnp.float32)[:, None] * freqs[None, :]
    cos = jnp.cos(ang)
    sin = jnp.sin(ang)
    return jnp.concatenate([cos, cos], axis=1), jnp.concatenate([sin, sin], axis=1)


def _tiles(b, s):
    t = b * s
    tk = min(512, s)
    return dict(ts=min(512, t), ts_up=min(1024, s), tk=tk, tb=min(512, s), tq=min(2048, s),
                attn_unroll=min(8, s // tk))


def _layer(x, w):
    b, s, _ = x.shape
    t = b * s
    tl = _tiles(b, s)
    x2d = x.reshape(t, D_MODEL)
    qkv, lat, act, gates = _proj(x2d, w["pre_g"], w["wqkv"], w["wlat"], w["wact"], w["wg"], w["bg"],
                                 tl["ts"])
    lat3 = lat.reshape(b, s, LAT_GR)
    cos, sin = _rope_tables(s)
    qt, kf, vt, qn2, kn2 = _mla_up(lat3, cos, sin, w["q_g"], w["kv_g"], w["wqn"], w["wqr"],
                                   w["wqrot"], w["wkn"], w["wv"], tl["ts_up"], tl["tk"])
    o_m = _attn(qt, kf, vt, qn2, kn2, tl["tq"], tl["attn_unroll"])
    o_f, o_b = _gla(qkv.reshape(b, s, QKV_W), gates.reshape(b, s, GATES_W), tl["tb"])
    y = _final(x2d, o_f.reshape(t, GLA_V_W), o_b.reshape(t, GLA_V_W), o_m.reshape(t, MLA_V_W),
               act, w["gla_g"], w["post_g"], w["w_a"], w["w_b"], w["w_o"], tl["ts"])
    return y.reshape(b, s, D_MODEL)


def kernel(x_prompt, x_sample, pre_norm_g, w_in, gla_wg2_fwd, gla_bg2_fwd, gla_wg2_bwd, gla_bg2_bwd,
           gla_out_norm_g, mla_q_norm_g, mla_w_uq, mla_kv_norm_g, mla_w_ukv, w_branch_a, w_branch_b,
           w_out, post_norm_g):
    depth = pre_norm_g.shape[0]
    y_prompt, y_sample = x_prompt, x_sample
    for l in range(depth):
        w = _prepare_weights(pre_norm_g[l], w_in[l], gla_wg2_fwd[l], gla_bg2_fwd[l], gla_wg2_bwd[l],
                             gla_bg2_bwd[l], gla_out_norm_g[l], mla_q_norm_g[l], mla_w_uq[l],
                             mla_kv_norm_g[l], mla_w_ukv[l], w_branch_a[l], w_branch_b[l],
                             w_out[l], post_norm_g[l])
        y_prompt = _layer(y_prompt, w)
        y_sample = _layer(y_sample, w)
    return (y_prompt, y_sample)
```

```python
import functools
import math

import jax
import jax.numpy as jnp
from jax import lax
from jax.experimental import pallas as pl
from jax.experimental.pallas import tpu as pltpu

D_MODEL = 1024
GLA_HEADS = 4
GLA_DK = 128
GLA_DV = 256
GLA_GATE_RANK = 16
GLA_GATE_NORM = 16.0
MLA_HEADS = 8
MLA_Q_RANK = 256
MLA_KV_RANK = 128
MLA_NOPE = 128
MLA_ROPE = 64
MLA_V = 128
ROPE_THETA = 10000.0
EPS = 1e-6

GLA_QK_W = GLA_HEADS * GLA_DK
GLA_V_W = GLA_HEADS * GLA_DV
MLA_V_W = MLA_HEADS * MLA_V
MLA_QK = MLA_NOPE + MLA_ROPE

V7X_LANES = 128
V7X_VMEM_LIMIT_BYTES = 56 * 1024 * 1024

QKV_W = 2 * GLA_QK_W + GLA_V_W
LAT_CQ = 0
LAT_CKV = MLA_Q_RANK
LAT_KR = LAT_CKV + MLA_KV_RANK
LAT_KROT = LAT_KR + V7X_LANES
LAT_GR = LAT_KROT + V7X_LANES
LAT_W = LAT_GR + V7X_LANES
ACT_W = 4 * D_MODEL
GATES_W = 2 * GLA_QK_W

GLA_KERNEL_CHUNK = 128
GLA_LOOKAHEAD = 4
GLA_EXP2_CLAMP = 115.0
LOG2E = 1.4426950408889634
ATTN_QSUB = 256
ATTN_LOOKAHEAD = 3
ATTN_DENOM_MIN = 2.0 ** -60
ATTN_DENOM_MAX = 2.0 ** 60
ATTN_VROWS = MLA_V + 16


def _cparams(semantics):
    return pltpu.CompilerParams(dimension_semantics=semantics,
                                vmem_limit_bytes=V7X_VMEM_LIMIT_BYTES)


def _const_spec(shape):
    return pl.BlockSpec(shape, lambda *_: (0,) * len(shape))


def _rms(x, g):
    return x * lax.rsqrt(jnp.mean(x * x, axis=-1, keepdims=True) + EPS) * g


def _sigmoid(x):
    return 1.0 / (1.0 + jnp.exp(-x))


def _bdot(a, b):
    return jnp.dot(a, b, preferred_element_type=jnp.float32)


def _bdot_nt(a, b):
    return lax.dot_general(a, b, (((1,), (1,)), ((), ())), preferred_element_type=jnp.float32)


def _bdot_tn(a, b):
    return lax.dot_general(a, b, (((0,), (0,)), ((), ())), preferred_element_type=jnp.float32)


def _proj_kernel(x_ref, g_ref, wqkv_ref, wlat_ref, wact_ref, wg_ref, bg_ref,
                 qkv_ref, lat_ref, act_ref, gates_ref):
    half = x_ref.shape[0] // 2
    for r in (slice(0, half), slice(half, 2 * half)):
        h = _rms(x_ref[r, :], g_ref[...]).astype(jnp.bfloat16)
        lat = _bdot(h, wlat_ref[...])
        xg = _bdot(lat[:, LAT_GR:].astype(jnp.bfloat16), wg_ref[...]) + bg_ref[...]
        qkv = _bdot(h, wqkv_ref[...])
        lat_ref[r, :] = lat[:, :LAT_GR]
        gates_ref[r, :] = ((jnp.minimum(xg, 0.0) - jnp.log(1.0 + jnp.exp(-jnp.abs(xg))))
                           * (LOG2E / GLA_GATE_NORM))
        z = _bdot(h, wact_ref[:, :2 * D_MODEL])
        qkv_ref[r, :GLA_QK_W] = (qkv[:, :GLA_QK_W] * (GLA_DK ** -0.5)).astype(jnp.bfloat16)
        qkv_ref[r, GLA_QK_W:] = qkv[:, GLA_QK_W:].astype(jnp.bfloat16)
        gate = _bdot(h, wact_ref[:, 2 * D_MODEL:])
        act_ref[r, :2 * D_MODEL] = (z * _sigmoid(z)).astype(jnp.bfloat16)
        act_ref[r, 2 * D_MODEL:] = _sigmoid(gate).astype(jnp.bfloat16)


def _proj(x2d, pre_g, wqkv, wlat, wact, wg, bg, ts):
    t = x2d.shape[0]
    return pl.pallas_call(
        _proj_kernel,
        grid=(t // ts,),
        in_specs=[pl.BlockSpec((ts, D_MODEL), lambda i: (i, 0)),
                  _const_spec((1, D_MODEL)),
                  _const_spec((D_MODEL, QKV_W)),
                  _const_spec((D_MODEL, LAT_W)),
                  _const_spec((D_MODEL, ACT_W)),
                  _const_spec((V7X_LANES, GATES_W)),
                  _const_spec((1, GATES_W))],
        out_specs=[pl.BlockSpec((ts, QKV_W), lambda i: (i, 0)),
                   pl.BlockSpec((ts, LAT_GR), lambda i: (i, 0)),
                   pl.BlockSpec((ts, ACT_W), lambda i: (i, 0)),
                   pl.BlockSpec((ts, GATES_W), lambda i: (i, 0))],
        out_shape=[jax.ShapeDtypeStruct((t, QKV_W), jnp.bfloat16),
                   jax.ShapeDtypeStruct((t, LAT_GR), jnp.float32),
                   jax.ShapeDtypeStruct((t, ACT_W), jnp.bfloat16),
                   jax.ShapeDtypeStruct((t, GATES_W), jnp.float32)],
        compiler_params=_cparams(("parallel",)),
        name="proj",
    )(x2d, pre_g, wqkv, wlat, wact, wg, bg)


def _mla_up_kernel(lat_ref, cos_ref, sin_ref, qg_ref, kvg_ref, wqn_ref, wqr_ref, wqrot_ref,
                   wkn_ref, wv_ref, qt_ref, kf_ref, vt_ref):
    lat = lat_ref[...]
    cos = cos_ref[...]
    sin = sin_ref[...]
    c_q = _rms(lat[:, LAT_CQ:LAT_CQ + MLA_Q_RANK], qg_ref[...]).astype(jnp.bfloat16)
    c_kv = _rms(lat[:, LAT_CKV:LAT_CKV + MLA_KV_RANK], kvg_ref[...]).astype(jnp.bfloat16)

    q_scale = (MLA_QK ** -0.5) * LOG2E
    qn_t = (_bdot(c_q, wqn_ref[...]) * q_scale).T
    cos_h = jnp.concatenate([cos, cos], axis=1)
    sin_h = jnp.concatenate([sin, sin], axis=1)
    cos_q = jnp.concatenate([cos_h] * (MLA_HEADS // 2), axis=1)
    sin_q = jnp.concatenate([sin_h] * (MLA_HEADS // 2), axis=1)
    q_pe = _bdot(c_q, wqr_ref[...]) * cos_q + _bdot(c_q, wqrot_ref[...]) * sin_q
    qpe_t = (q_pe * q_scale).T

    k_pe = (lat[:, LAT_KR:LAT_KR + MLA_ROPE] * cos
            + lat[:, LAT_KROT:LAT_KROT + MLA_ROPE] * sin).astype(jnp.bfloat16)
    kn = _bdot(c_kv, wkn_ref[...]).astype(jnp.bfloat16)
    v_t = _bdot(c_kv, wv_ref[...]).T

    n_kc, tk = vt_ref.shape[1], vt_ref.shape[3]
    tile_row = lax.broadcasted_iota(jnp.int32, (ATTN_VROWS - MLA_V, tk), 0)
    ones_tile = (tile_row == 0).astype(jnp.bfloat16)
    for h in range(MLA_HEADS):
        qt_ref[h, :MLA_NOPE, :] = qn_t[h * MLA_NOPE:(h + 1) * MLA_NOPE, :].astype(jnp.bfloat16)
        qt_ref[h, MLA_NOPE:, :] = qpe_t[h * MLA_ROPE:(h + 1) * MLA_ROPE, :].astype(jnp.bfloat16)
        kf_ref[h, :, :MLA_NOPE] = kn[:, h * MLA_NOPE:(h + 1) * MLA_NOPE]
        kf_ref[h, :, MLA_NOPE:] = k_pe
        for u in range(n_kc):
            vt_ref[h, u, :MLA_V, :] = v_t[h * MLA_V:(h + 1) * MLA_V,
                                          u * tk:(u + 1) * tk].astype(jnp.bfloat16)
            vt_ref[h, u, MLA_V:, :] = ones_tile


def _mla_up(lat3, cos, sin, qg, kvg, wqn, wqr, wqrot, wkn, wv, ts, tk):
    b, s, _ = lat3.shape
    hh = MLA_HEADS
    return pl.pallas_call(
        _mla_up_kernel,
        grid=(b, s // ts),
        in_specs=[pl.BlockSpec((None, ts, LAT_GR), lambda i, j: (i, j, 0)),
                  pl.BlockSpec((ts, MLA_ROPE), lambda i, j: (j, 0)),
                  pl.BlockSpec((ts, MLA_ROPE), lambda i, j: (j, 0)),
                  _const_spec((1, MLA_Q_RANK)),
                  _const_spec((1, MLA_KV_RANK)),
                  _const_spec((MLA_Q_RANK, hh * MLA_NOPE)),
                  _const_spec((MLA_Q_RANK, hh * MLA_ROPE)),
                  _const_spec((MLA_Q_RANK, hh * MLA_ROPE)),
                  _const_spec((MLA_KV_RANK, hh * MLA_NOPE)),
                  _const_spec((MLA_KV_RANK, hh * MLA_V))],
        out_specs=[pl.BlockSpec((None, hh, MLA_QK, ts), lambda i, j: (i, 0, 0, j)),
                   pl.BlockSpec((None, hh, ts, MLA_QK), lambda i, j: (i, 0, j, 0)),
                   pl.BlockSpec((None, hh, ts // tk, ATTN_VROWS, tk),
                                lambda i, j: (i, 0, j, 0, 0))],
        out_shape=[jax.ShapeDtypeStruct((b, hh, MLA_QK, s), jnp.bfloat16),
                   jax.ShapeDtypeStruct((b, hh, s, MLA_QK), jnp.bfloat16),
                   jax.ShapeDtypeStruct((b, hh, s // tk, ATTN_VROWS, tk), jnp.bfloat16)],
        compiler_params=_cparams(("parallel", "parallel")),
        name="mla_up",
    )(lat3, cos, sin, qg, kvg, wqn, wqr, wqrot, wkn, wv)


def _attn_kernel(qt_ref, k_ref, vt_ref, o_ref, m_ref, acc_ref, *, unroll):
    n_chunks, _, tk = vt_ref.shape
    n_sub = qt_ref.shape[1] // ATTN_QSUB

    def scores(c, j):
        rows = pl.ds(pl.multiple_of(c * tk, tk), tk)
        return _bdot(k_ref[rows, :], qt_ref[:, j * ATTN_QSUB:(j + 1) * ATTN_QSUB])

    def run(chain_tail):
        acc_ref[...] = jnp.zeros_like(acc_ref)

        def group(g, carry):
            chains = [(g * unroll + u, j) for u in range(unroll) for j in range(n_sub)]
            pending = [scores(*ch) for ch in chains[:ATTN_LOOKAHEAD]]
            for idx, (c, j) in enumerate(chains):
                s = pending.pop(0)
                if idx + ATTN_LOOKAHEAD < len(chains):
                    pending.append(scores(*chains[idx + ATTN_LOOKAHEAD]))
                chain_tail(s, c, slice(j * ATTN_QSUB, (j + 1) * ATTN_QSUB))
            return carry

        n_groups = n_chunks // unroll
        if n_groups == 1:
            group(0, 0)
        else:
            lax.fori_loop(0, n_groups, group, 0)

    def plain_tail(s, c, cols):
        acc_ref[:, cols] = acc_ref[:, cols] + _bdot(vt_ref[c], jnp.exp2(s).astype(jnp.bfloat16))

    def online_tail(s, c, cols):
        m_old = m_ref[:, cols]
        m_new = jnp.maximum(m_old, jnp.max(s, axis=0, keepdims=True))
        alpha = jnp.exp2(m_old - m_new)
        p = jnp.exp2(s - m_new)
        acc_ref[:, cols] = alpha * acc_ref[:, cols] + _bdot(vt_ref[c], p.astype(jnp.bfloat16))
        m_ref[:, cols] = m_new

    run(plain_tail)
    denom = acc_ref[MLA_V:MLA_V + 1, :]
    in_range = (denom >= ATTN_DENOM_MIN) & (denom <= ATTN_DENOM_MAX)
    n_bad = jnp.sum(jnp.where(in_range, 0.0, 1.0))

    @pl.when(n_bad > 0.0)
    def _():
        m_ref[...] = jnp.full_like(m_ref, -jnp.inf)
        run(online_tail)

    o_ref[...] = (acc_ref[:MLA_V, :] / acc_ref[MLA_V:MLA_V + 1, :]).T.astype(o_ref.dtype)


def _attn(qt, kf, vt, tq, unroll):
    b, hh, _, s = qt.shape
    n_chunks, _, tk = vt.shape[2:]
    return pl.pallas_call(
        functools.partial(_attn_kernel, unroll=unroll),
        grid=(b, hh, s // tq),
        in_specs=[pl.BlockSpec((None, None, MLA_QK, tq), lambda i, h, q: (i, h, 0, q)),
                  pl.BlockSpec((None, None, s, MLA_QK), lambda i, h, q: (i, h, 0, 0)),
                  pl.BlockSpec((None, None, n_chunks, ATTN_VROWS, tk),
                               lambda i, h, q: (i, h, 0, 0, 0))],
        out_specs=pl.BlockSpec((None, tq, MLA_V), lambda i, h, q: (i, q, h)),
        out_shape=jax.ShapeDtypeStruct((b, s, MLA_V_W), jnp.bfloat16),
        scratch_shapes=[pltpu.VMEM((1, tq), jnp.float32),
                        pltpu.VMEM((ATTN_VROWS, tq), jnp.float32)],
        compiler_params=_cparams(("parallel", "parallel", "parallel")),
        name="attn",
    )(qt, kf, vt)


def _split_bf16(x, pieces):
    out = []
    for _ in range(pieces):
        term = x.astype(jnp.bfloat16)
        out.append(term)
        x = x - term.astype(jnp.float32)
    return out


def _gla_cumsum(g, reverse, pieces):
    c = GLA_KERNEL_CHUNK
    row = lax.broadcasted_iota(jnp.int32, (c, c), 0)
    col = lax.broadcasted_iota(jnp.int32, (c, c), 1)
    tri = ((col >= row) if reverse else (col <= row)).astype(jnp.bfloat16)
    return _bdot(jnp.concatenate([tri] * pieces, axis=1),
                 jnp.concatenate(_split_bf16(g, pieces), axis=0))


def _gla_scores_factorised(q, k, b, keep):
    c = GLA_KERNEL_CHUNK
    d = b - b[c // 2:c // 2 + 1, :]
    q_in = (q * jnp.exp2(jnp.minimum(d, GLA_EXP2_CLAMP))).astype(jnp.bfloat16)
    k_in = (k * jnp.exp2(jnp.minimum(-d, GLA_EXP2_CLAMP))).astype(jnp.bfloat16)
    zeros = jnp.zeros((c, GLA_DK), jnp.bfloat16)
    keep2 = jnp.concatenate([keep, keep], axis=1)
    scores = []
    for hp in range(GLA_HEADS // 2):
        k0 = k_in[:, (2 * hp) * GLA_DK:(2 * hp + 1) * GLA_DK]
        k1 = k_in[:, (2 * hp + 1) * GLA_DK:(2 * hp + 2) * GLA_DK]
        k_diag = jnp.concatenate([jnp.concatenate([k0, zeros], axis=1),
                                  jnp.concatenate([zeros, k1], axis=1)], axis=0)
        a2 = _bdot_nt(q_in[:, 2 * hp * GLA_DK:(2 * hp + 2) * GLA_DK], k_diag)
        a2 = jnp.where(keep2, a2, 0.0).astype(jnp.bfloat16)
        scores += [a2[:, :c], a2[:, c:]]
    return scores


def _gla_scores_direct(q, k, b, keep, b_scr, k_scr):
    c = GLA_KERNEL_CHUNK
    b_scr[...] = b
    k_scr[...] = k
    col = lax.broadcasted_iota(jnp.int32, (c, c), 1)

    def key_row(s, accs):
        e = jnp.exp2(jnp.minimum(b - b_scr[pl.ds(s, 1), :], 0.0)) * (q * k_scr[pl.ds(s, 1), :])
        return tuple(
            jnp.where(col == s,
                      jnp.sum(e[:, h * GLA_DK:(h + 1) * GLA_DK], axis=1, keepdims=True), acc)
            for h, acc in enumerate(accs))

    accs = lax.fori_loop(0, c, key_row,
                         tuple(jnp.zeros((c, c), jnp.float32) for _ in range(GLA_HEADS)))
    return [jnp.where(keep, a, 0.0).astype(jnp.bfloat16) for a in accs]


def _gla_chunk_local(qkv_ref, b, rows, reverse, direct_scratch):
    c = GLA_KERNEL_CHUNK
    row = lax.broadcasted_iota(jnp.int32, (c, c), 0)
    col = lax.broadcasted_iota(jnp.int32, (c, c), 1)
    keep = (col > row) if reverse else (col <= row)
    last = 0 if reverse else c - 1
    b_last = b[last:last + 1, :]
    q = qkv_ref[rows, 0:GLA_QK_W].astype(jnp.float32)
    k = qkv_ref[rows, GLA_QK_W:2 * GLA_QK_W].astype(jnp.float32)
    q_st = (q * jnp.exp2(b)).astype(jnp.bfloat16)
    k_st = (k * jnp.exp2(b_last - b)).astype(jnp.bfloat16)
    e_last = jnp.exp2(b_last)
    if direct_scratch is None:
        scores = _gla_scores_factorised(q, k, b, keep)
    else:
        scores = _gla_scores_direct(q, k, b, keep, *direct_scratch)
    heads = []
    for h in range(GLA_HEADS):
        ks = slice(h * GLA_DK, (h + 1) * GLA_DK)
        v = qkv_ref[rows, 2 * GLA_QK_W + h * GLA_DV:2 * GLA_QK_W + (h + 1) * GLA_DV]
        upd = _bdot_tn(k_st[:, ks], v)
        decay_col = jnp.broadcast_to(e_last[:, ks], (GLA_DK, GLA_DK)).T
        decay = jnp.concatenate([decay_col] * (GLA_DV // GLA_DK), axis=1)
        heads.append((jnp.concatenate([scores[h], q_st[:, ks]], axis=1), v, upd, decay))
    return heads


def _gla_chunk_state(heads, rows, o_ref, state_ref):
    for h, (lhs, v, upd, decay) in enumerate(heads):
        st = state_ref[h]
        o = _bdot(lhs, jnp.concatenate([v, st.astype(jnp.bfloat16)], axis=0))
        o_ref[rows, h * GLA_DV:(h + 1) * GLA_DV] = o.astype(o_ref.dtype)
        state_ref[h] = st * decay + upd


def _gla_block(qkv_f_ref, g_f_ref, qkv_b_ref, g_b_ref, of_ref, ob_ref, sf_ref, sb_ref,
               direct_scratch):
    c = GLA_KERNEL_CHUNK
    n_chunks = qkv_f_ref.shape[0] // c
    work = []
    for ci in range(n_chunks):
        rows_f = slice(ci * c, (ci + 1) * c)
        rows_b = slice((n_chunks - 1 - ci) * c, (n_chunks - ci) * c)
        work.append((qkv_f_ref, g_f_ref, rows_f, False, of_ref, sf_ref))
        work.append((qkv_b_ref, g_b_ref, rows_b, True, ob_ref, sb_ref))
    pieces = 2 if direct_scratch is None else 3
    sums = [_gla_cumsum(g_ref[rows, :], rev, pieces) for (_, g_ref, rows, rev, _, _) in work]
    local = lambda i: _gla_chunk_local(work[i][0], sums[i], work[i][2], work[i][3], direct_scratch)
    pending = [local(i) for i in range(min(GLA_LOOKAHEAD, len(work)))]
    for i, (_, _, rows, _, o_ref, state_ref) in enumerate(work):
        heads = pending.pop(0)
        if i + GLA_LOOKAHEAD < len(work):
            pending.append(local(i + GLA_LOOKAHEAD))
        _gla_chunk_state(heads, rows, o_ref, state_ref)


def _gla_kernel(qkv_f_ref, g_f_ref, qkv_b_ref, g_b_ref, of_ref, ob_ref, sf_ref, sb_ref,
                b_scr, k_scr):
    @pl.when(pl.program_id(1) == 0)
    def _():
        sf_ref[...] = jnp.zeros_like(sf_ref)
        sb_ref[...] = jnp.zeros_like(sb_ref)

    refs = (qkv_f_ref, g_f_ref, qkv_b_ref, g_b_ref, of_ref, ob_ref, sf_ref, sb_ref)
    g_min = jnp.minimum(jnp.min(g_f_ref[...]), jnp.min(g_b_ref[...]))
    steep = g_min * (GLA_KERNEL_CHUNK // 2) < -GLA_EXP2_CLAMP
    lax.cond(steep,
             lambda: _gla_block(*refs, (b_scr, k_scr)),
             lambda: _gla_block(*refs, None))


def _gla(qkv3, gates3, tb):
    b, s, _ = qkv3.shape
    nb = s // tb
    fwd = lambda i, j: (i, j, 0)
    bwd = lambda i, j: (i, nb - 1 - j, 0)
    return pl.pallas_call(
        _gla_kernel,
        grid=(b, nb),
        in_specs=[pl.BlockSpec((None, tb, QKV_W), fwd),
                  pl.BlockSpec((None, tb, GLA_QK_W), fwd),
                  pl.BlockSpec((None, tb, QKV_W), bwd),
                  pl.BlockSpec((None, tb, GLA_QK_W), lambda i, j: (i, nb - 1 - j, 1))],
        out_specs=[pl.BlockSpec((None, tb, GLA_V_W), fwd),
                   pl.BlockSpec((None, tb, GLA_V_W), bwd)],
        out_shape=[jax.ShapeDtypeStruct((b, s, GLA_V_W), jnp.bfloat16),
                   jax.ShapeDtypeStruct((b, s, GLA_V_W), jnp.bfloat16)],
        scratch_shapes=[pltpu.VMEM((GLA_HEADS, GLA_DK, GLA_DV), jnp.float32),
                        pltpu.VMEM((GLA_HEADS, GLA_DK, GLA_DV), jnp.float32),
                        pltpu.VMEM((GLA_KERNEL_CHUNK, GLA_QK_W), jnp.float32),
                        pltpu.VMEM((GLA_KERNEL_CHUNK, GLA_QK_W), jnp.float32)],
        compiler_params=_cparams(("parallel", "arbitrary")),
        name="gla",
    )(qkv3, gates3, qkv3, gates3)


def _final_kernel(x_ref, of_ref, ob_ref, om_ref, act_ref, gg_ref, pg_ref, wa_ref, wb_ref, wo_ref,
                  y_ref):
    o_sum = of_ref[...].astype(jnp.float32) + ob_ref[...].astype(jnp.float32)
    gg = gg_ref[...]
    o_a = jnp.concatenate(
        [_rms(o_sum[:, h * GLA_DV:(h + 1) * GLA_DV], gg) for h in range(GLA_HEADS)], axis=1)
    y_a = (o_a * act_ref[:, 0:D_MODEL].astype(jnp.float32)).astype(jnp.bfloat16)
    y_b = om_ref[...] * act_ref[:, D_MODEL:2 * D_MODEL]
    merged = (act_ref[:, 2 * D_MODEL:3 * D_MODEL].astype(jnp.float32) * _bdot(y_a, wa_ref[...])
              + act_ref[:, 3 * D_MODEL:].astype(jnp.float32) * _bdot(y_b, wb_ref[...]))
    out = _bdot(merged.astype(jnp.bfloat16), wo_ref[...])
    y_ref[...] = x_ref[...] + _rms(out, pg_ref[...])


def _final(x2d, o_f, o_b, o_m, act, gg, pg, wa, wb, wo, ts):
    t = x2d.shape[0]
    tok = lambda w: pl.BlockSpec((ts, w), lambda i: (i, 0))
    return pl.pallas_call(
        _final_kernel,
        grid=(t // ts,),
        in_specs=[tok(D_MODEL), tok(GLA_V_W), tok(GLA_V_W), tok(MLA_V_W), tok(ACT_W),
                  _const_spec((1, GLA_DV)), _const_spec((1, D_MODEL)),
                  _const_spec((GLA_V_W, D_MODEL)), _const_spec((MLA_V_W, D_MODEL)),
                  _const_spec((D_MODEL, D_MODEL))],
        out_specs=tok(D_MODEL),
        out_shape=jax.ShapeDtypeStruct((t, D_MODEL), jnp.float32),
        compiler_params=_cparams(("parallel",)),
        name="final",
    )(x2d, o_f, o_b, o_m, act, gg, pg, wa, wb, wo)


def _rotate_half_cols(w):
    half = MLA_ROPE // 2
    w3 = w.reshape(w.shape[0], -1, MLA_ROPE)
    return jnp.concatenate([-w3[..., half:], w3[..., :half]], axis=-1).reshape(w.shape)


def _pad_cols(w, width):
    return jnp.pad(w, ((0, 0), (0, width - w.shape[1])))


def _prepare_weights(pre_g, w_in, wg2_f, bg2_f, wg2_b, bg2_b, gla_g, q_g, w_uq, kv_g, w_ukv,
                     w_a, w_b, w_o, post_g):
    bf = jnp.bfloat16
    sizes = (GLA_QK_W, GLA_QK_W, GLA_V_W, GLA_GATE_RANK, GLA_GATE_RANK, GLA_V_W,
             MLA_Q_RANK, MLA_KV_RANK, MLA_ROPE, MLA_V_W, D_MODEL, D_MODEL)
    offs = [0]
    for sz in sizes:
        offs.append(offs[-1] + sz)
    (w_q, w_k, w_v, w_grf, w_grb, w_za, w_cq, w_ckv, w_kr, w_zb, w_ga, w_gb) = [
        w_in[:, offs[i]:offs[i + 1]] for i in range(len(sizes))]
    wqkv = jnp.concatenate([w_q, w_k, w_v], axis=1).astype(bf)
    wlat = jnp.concatenate([
        w_cq, w_ckv, _pad_cols(w_kr, V7X_LANES), _pad_cols(_rotate_half_cols(w_kr), V7X_LANES),
        _pad_cols(jnp.concatenate([w_grf, w_grb], axis=1), V7X_LANES)], axis=1).astype(bf)
    wact = jnp.concatenate([w_za, w_zb, w_ga, w_gb], axis=1).astype(bf)

    wg = jnp.zeros((V7X_LANES, GATES_W), jnp.float32)
    wg = wg.at[0:GLA_GATE_RANK, :GLA_QK_W].set(wg2_f)
    wg = wg.at[GLA_GATE_RANK:2 * GLA_GATE_RANK, GLA_QK_W:].set(wg2_b).astype(bf)
    bg = jnp.concatenate([bg2_f, bg2_b])[None, :]

    uq = w_uq.reshape(MLA_Q_RANK, MLA_HEADS, MLA_QK)
    wqn = uq[:, :, :MLA_NOPE].reshape(MLA_Q_RANK, MLA_HEADS * MLA_NOPE)
    wqr = uq[:, :, MLA_NOPE:].reshape(MLA_Q_RANK, MLA_HEADS * MLA_ROPE)
    ukv = w_ukv.reshape(MLA_KV_RANK, MLA_HEADS, MLA_NOPE + MLA_V)
    wkn = ukv[:, :, :MLA_NOPE].reshape(MLA_KV_RANK, MLA_HEADS * MLA_NOPE)
    wv = ukv[:, :, MLA_NOPE:].reshape(MLA_KV_RANK, MLA_HEADS * MLA_V)
    return dict(
        pre_g=pre_g[None, :], wqkv=wqkv, wlat=wlat, wact=wact,
        wg=wg, bg=bg,
        gla_g=gla_g[None, :], q_g=q_g[None, :], kv_g=kv_g[None, :],
        wqn=wqn.astype(bf), wqr=wqr.astype(bf), wqrot=_rotate_half_cols(wqr).astype(bf),
        wkn=wkn.astype(bf), wv=wv.astype(bf),
        w_a=w_a.astype(bf), w_b=w_b.astype(bf), w_o=w_o.astype(bf), post_g=post_g[None, :])


def _rope_tables(s):
    half = MLA_ROPE // 2
    freqs = ROPE_THETA ** (-jnp.arange(half, dtype=jnp.float32) / half)
    ang = jnp.arange(s).astype(jnp.float32)[:, None] * freqs[None, :]
    cos = jnp.cos(ang)
    sin = jnp.sin(ang)
    return jnp.concatenate([cos, cos], axis=1), jnp.concatenate([sin, sin], axis=1)


def _tiles(b, s):
    t = b * s
    tk = min(512, s)
    return dict(ts=min(512, t), ts_up=min(1024, s), tk=tk, tb=min(512, s), tq=min(2048, s),
                attn_unroll=min(8, s // tk))


def _layer(x, w):
    b, s, _ = x.shape
    t = b * s
    tl = _tiles(b, s)
    x2d = x.reshape(t, D_MODEL)
    qkv, lat, act, gates = _proj(x2d, w["pre_g"], w["wqkv"], w["wlat"], w["wact"], w["wg"], w["bg"],
                                 tl["ts"])
    lat3 = lat.reshape(b, s, LAT_GR)
    cos, sin = _rope_tables(s)
    qt, kf, vt = _mla_up(lat3, cos, sin, w["q_g"], w["kv_g"], w["wqn"], w["wqr"], w["wqrot"],
                         w["wkn"], w["wv"], tl["ts_up"], tl["tk"])
    o_m = _attn(qt, kf, vt, tl["tq"], tl["attn_unroll"])
    o_f, o_b = _gla(qkv.reshape(b, s, QKV_W), gates.reshape(b, s, GATES_W), tl["tb"])
    y = _final(x2d, o_f.reshape(t, GLA_V_W), o_b.reshape(t, GLA_V_W), o_m.reshape(t, MLA_V_W),
               act, w["gla_g"], w["post_g"], w["w_a"], w["w_b"], w["w_o"], tl["ts"])
    return y.reshape(b, s, D_MODEL)


def kernel(x_prompt, x_sample, pre_norm_g, w_in, gla_wg2_fwd, gla_bg2_fwd, gla_wg2_bwd, gla_bg2_bwd,
           gla_out_norm_g, mla_q_norm_g, mla_w_uq, mla_kv_norm_g, mla_w_ukv, w_branch_a, w_branch_b,
           w_out, post_norm_g):
    depth = pre_norm_g.shape[0]
    y_prompt, y_sample = x_prompt, x_sample
    for l in range(depth):
        w = _prepare_weights(pre_norm_g[l], w_in[l], gla_wg2_fwd[l], gla_bg2_fwd[l], gla_wg2_bwd[l],
                             gla_bg2_bwd[l], gla_out_norm_g[l], mla_q_norm_g[l], mla_w_uq[l],
                             mla_kv_norm_g[l], mla_w_ukv[l], w_branch_a[l], w_branch_b[l],
                             w_out[l], post_norm_g[l])
        y_prompt = _layer(y_prompt, w)
        y_sample = _layer(y_sample, w)
    return (y_prompt, y_sample)
```

```python
import functools
import math

import jax
import jax.numpy as jnp
from jax import lax
from jax.experimental import pallas as pl
from jax.experimental.pallas import tpu as pltpu

D_MODEL = 1024
GLA_HEADS = 4
GLA_DK = 128
GLA_DV = 256
GLA_GATE_RANK = 16
GLA_GATE_NORM = 16.0
MLA_HEADS = 8
MLA_Q_RANK = 256
MLA_KV_RANK = 128
MLA_NOPE = 128
MLA_ROPE = 64
MLA_V = 128
ROPE_THETA = 10000.0
EPS = 1e-6

GLA_QK_W = GLA_HEADS * GLA_DK
GLA_V_W = GLA_HEADS * GLA_DV
MLA_V_W = MLA_HEADS * MLA_V
MLA_QK = MLA_NOPE + MLA_ROPE

V7X_LANES = 128
V7X_VMEM_LIMIT_BYTES = 56 * 1024 * 1024

QKV_W = 2 * GLA_QK_W + GLA_V_W
LAT_CQ = 0
LAT_CKV = MLA_Q_RANK
LAT_KR = LAT_CKV + MLA_KV_RANK
LAT_KROT = LAT_KR + V7X_LANES
LAT_GR = LAT_KROT + V7X_LANES
LAT_W = LAT_GR + V7X_LANES
ACT_W = 4 * D_MODEL
GATES_W = 2 * GLA_QK_W

GLA_KERNEL_CHUNK = 128
GLA_LOOKAHEAD = 4
GLA_EXP2_CLAMP = 115.0
LOG2E = 1.4426950408889634
ATTN_QSUB = 256
ATTN_LOOKAHEAD = 3
ATTN_DENOM_MIN = 2.0 ** -60
ATTN_DENOM_MAX = 2.0 ** 60


def _cparams(semantics):
    return pltpu.CompilerParams(dimension_semantics=semantics,
                                vmem_limit_bytes=V7X_VMEM_LIMIT_BYTES)


def _const_spec(shape):
    return pl.BlockSpec(shape, lambda *_: (0,) * len(shape))


def _rms(x, g):
    return x * lax.rsqrt(jnp.mean(x * x, axis=-1, keepdims=True) + EPS) * g


def _sigmoid(x):
    return 1.0 / (1.0 + jnp.exp(-x))


def _bdot(a, b):
    return jnp.dot(a, b, preferred_element_type=jnp.float32)


def _bdot_nt(a, b):
    return lax.dot_general(a, b, (((1,), (1,)), ((), ())), preferred_element_type=jnp.float32)


def _bdot_tn(a, b):
    return lax.dot_general(a, b, (((0,), (0,)), ((), ())), preferred_element_type=jnp.float32)


def _proj_kernel(x_ref, g_ref, wqkv_ref, wlat_ref, wact_ref, wg_ref, bg_ref,
                 qkv_ref, lat_ref, act_ref, gates_ref, gmin_ref):
    half = x_ref.shape[0] // 2
    gate_mins = []
    for r in (slice(0, half), slice(half, 2 * half)):
        h = _rms(x_ref[r, :], g_ref[...]).astype(jnp.bfloat16)
        lat = _bdot(h, wlat_ref[...])
        xg = _bdot(lat[:, LAT_GR:].astype(jnp.bfloat16), wg_ref[...]) + bg_ref[...]
        qkv = _bdot(h, wqkv_ref[...])
        lat_ref[r, :] = lat[:, :LAT_GR]
        gates = ((jnp.minimum(xg, 0.0) - jnp.log(1.0 + jnp.exp(-jnp.abs(xg))))
                 * (LOG2E / GLA_GATE_NORM))
        gates_ref[r, :] = gates
        by_sublane = jnp.min(gates.reshape(half // 8, 8, GATES_W), axis=0)
        gate_mins += [by_sublane[:, i * V7X_LANES:(i + 1) * V7X_LANES]
                      for i in range(GATES_W // V7X_LANES)]
        z = _bdot(h, wact_ref[:, :2 * D_MODEL])
        qkv_ref[r, :GLA_QK_W] = (qkv[:, :GLA_QK_W] * (GLA_DK ** -0.5)).astype(jnp.bfloat16)
        qkv_ref[r, GLA_QK_W:] = qkv[:, GLA_QK_W:].astype(jnp.bfloat16)
        gate = _bdot(h, wact_ref[:, 2 * D_MODEL:])
        act_ref[r, :2 * D_MODEL] = (z * _sigmoid(z)).astype(jnp.bfloat16)
        act_ref[r, 2 * D_MODEL:] = _sigmoid(gate).astype(jnp.bfloat16)
    gmin_ref[...] = functools.reduce(jnp.minimum, gate_mins)


def _proj(x2d, pre_g, wqkv, wlat, wact, wg, bg, ts):
    t = x2d.shape[0]
    return pl.pallas_call(
        _proj_kernel,
        grid=(t // ts,),
        in_specs=[pl.BlockSpec((ts, D_MODEL), lambda i: (i, 0)),
                  _const_spec((1, D_MODEL)),
                  _const_spec((D_MODEL, QKV_W)),
                  _const_spec((D_MODEL, LAT_W)),
                  _const_spec((D_MODEL, ACT_W)),
                  _const_spec((V7X_LANES, GATES_W)),
                  _const_spec((1, GATES_W))],
        out_specs=[pl.BlockSpec((ts, QKV_W), lambda i: (i, 0)),
                   pl.BlockSpec((ts, LAT_GR), lambda i: (i, 0)),
                   pl.BlockSpec((ts, ACT_W), lambda i: (i, 0)),
                   pl.BlockSpec((ts, GATES_W), lambda i: (i, 0)),
                   pl.BlockSpec((None, 8, V7X_LANES), lambda i: (i, 0, 0))],
        out_shape=[jax.ShapeDtypeStruct((t, QKV_W), jnp.bfloat16),
                   jax.ShapeDtypeStruct((t, LAT_GR), jnp.float32),
                   jax.ShapeDtypeStruct((t, ACT_W), jnp.bfloat16),
                   jax.ShapeDtypeStruct((t, GATES_W), jnp.float32),
                   jax.ShapeDtypeStruct((t // ts, 8, V7X_LANES), jnp.float32)],
        compiler_params=_cparams(("parallel",)),
        name="proj",
    )(x2d, pre_g, wqkv, wlat, wact, wg, bg)


def _mla_up_kernel(lat_ref, cos_ref, sin_ref, qg_ref, kvg_ref, wqn_ref, wqr_ref, wqrot_ref,
                   wkn_ref, wv_ref, qt_ref, kf_ref, vt_ref):
    lat = lat_ref[...]
    cos = cos_ref[...]
    sin = sin_ref[...]
    c_q = _rms(lat[:, LAT_CQ:LAT_CQ + MLA_Q_RANK], qg_ref[...]).astype(jnp.bfloat16)
    c_kv = _rms(lat[:, LAT_CKV:LAT_CKV + MLA_KV_RANK], kvg_ref[...]).astype(jnp.bfloat16)

    q_scale = (MLA_QK ** -0.5) * LOG2E
    qn_t = (_bdot(c_q, wqn_ref[...]) * q_scale).T
    cos_h = jnp.concatenate([cos, cos], axis=1)
    sin_h = jnp.concatenate([sin, sin], axis=1)
    cos_q = jnp.concatenate([cos_h] * (MLA_HEADS // 2), axis=1)
    sin_q = jnp.concatenate([sin_h] * (MLA_HEADS // 2), axis=1)
    q_pe = _bdot(c_q, wqr_ref[...]) * cos_q + _bdot(c_q, wqrot_ref[...]) * sin_q
    qpe_t = (q_pe * q_scale).T

    k_pe = (lat[:, LAT_KR:LAT_KR + MLA_ROPE] * cos
            + lat[:, LAT_KROT:LAT_KROT + MLA_ROPE] * sin).astype(jnp.bfloat16)
    kn = _bdot(c_kv, wkn_ref[...]).astype(jnp.bfloat16)
    v_t = _bdot(c_kv, wv_ref[...]).T

    n_kc, tk = vt_ref.shape[1], vt_ref.shape[3]
    for h in range(MLA_HEADS):
        qt_ref[h, :MLA_NOPE, :] = qn_t[h * MLA_NOPE:(h + 1) * MLA_NOPE, :].astype(jnp.bfloat16)
        qt_ref[h, MLA_NOPE:, :] = qpe_t[h * MLA_ROPE:(h + 1) * MLA_ROPE, :].astype(jnp.bfloat16)
        kf_ref[h, :, :MLA_NOPE] = kn[:, h * MLA_NOPE:(h + 1) * MLA_NOPE]
        kf_ref[h, :, MLA_NOPE:] = k_pe
        for u in range(n_kc):
            vt_ref[h, u] = v_t[h * MLA_V:(h + 1) * MLA_V, u * tk:(u + 1) * tk].astype(jnp.bfloat16)


def _mla_up(lat3, cos, sin, qg, kvg, wqn, wqr, wqrot, wkn, wv, ts, tk):
    b, s, _ = lat3.shape
    hh = MLA_HEADS
    return pl.pallas_call(
        _mla_up_kernel,
        grid=(b, s // ts),
        in_specs=[pl.BlockSpec((None, ts, LAT_GR), lambda i, j: (i, j, 0)),
                  pl.BlockSpec((ts, MLA_ROPE), lambda i, j: (j, 0)),
                  pl.BlockSpec((ts, MLA_ROPE), lambda i, j: (j, 0)),
                  _const_spec((1, MLA_Q_RANK)),
                  _const_spec((1, MLA_KV_RANK)),
                  _const_spec((MLA_Q_RANK, hh * MLA_NOPE)),
                  _const_spec((MLA_Q_RANK, hh * MLA_ROPE)),
                  _const_spec((MLA_Q_RANK, hh * MLA_ROPE)),
                  _const_spec((MLA_KV_RANK, hh * MLA_NOPE)),
                  _const_spec((MLA_KV_RANK, hh * MLA_V))],
        out_specs=[pl.BlockSpec((None, hh, MLA_QK, ts), lambda i, j: (i, 0, 0, j)),
                   pl.BlockSpec((None, hh, ts, MLA_QK), lambda i, j: (i, 0, j, 0)),
                   pl.BlockSpec((None, hh, ts // tk, MLA_V, tk), lambda i, j: (i, 0, j, 0, 0))],
        out_shape=[jax.ShapeDtypeStruct((b, hh, MLA_QK, s), jnp.bfloat16),
                   jax.ShapeDtypeStruct((b, hh, s, MLA_QK), jnp.bfloat16),
                   jax.ShapeDtypeStruct((b, hh, s // tk, MLA_V, tk), jnp.bfloat16)],
        compiler_params=_cparams(("parallel", "parallel")),
        name="mla_up",
    )(lat3, cos, sin, qg, kvg, wqn, wqr, wqrot, wkn, wv)


def _attn_kernel(qt_ref, k_ref, vt_ref, o_ref, m_ref, l_ref, acc_ref, *, unroll):
    n_chunks, _, tk = vt_ref.shape
    n_sub = qt_ref.shape[1] // ATTN_QSUB

    def scores(c, j):
        rows = pl.ds(pl.multiple_of(c * tk, tk), tk)
        return _bdot(k_ref[rows, :], qt_ref[:, j * ATTN_QSUB:(j + 1) * ATTN_QSUB])

    def run(chain_tail):
        acc_ref[...] = jnp.zeros_like(acc_ref)
        l_ref[...] = jnp.zeros_like(l_ref)

        def group(g, carry):
            chains = [(g * unroll + u, j) for u in range(unroll) for j in range(n_sub)]
            pending = [scores(*ch) for ch in chains[:ATTN_LOOKAHEAD]]
            for idx, (c, j) in enumerate(chains):
                s = pending.pop(0)
                if idx + ATTN_LOOKAHEAD < len(chains):
                    pending.append(scores(*chains[idx + ATTN_LOOKAHEAD]))
                chain_tail(s, c, slice(j * ATTN_QSUB, (j + 1) * ATTN_QSUB))
            return carry

        n_groups = n_chunks // unroll
        if n_groups == 1:
            group(0, 0)
        else:
            lax.fori_loop(0, n_groups, group, 0)

    def plain_tail(s, c, cols):
        p = jnp.exp2(s)
        l_ref[:, cols] = l_ref[:, cols] + jnp.sum(p, axis=0, keepdims=True)
        acc_ref[:, cols] = acc_ref[:, cols] + _bdot(vt_ref[c], p.astype(jnp.bfloat16))

    def online_tail(s, c, cols):
        m_old = m_ref[:, cols]
        m_new = jnp.maximum(m_old, jnp.max(s, axis=0, keepdims=True))
        alpha = jnp.exp2(m_old - m_new)
        p = jnp.exp2(s - m_new)
        l_ref[:, cols] = alpha * l_ref[:, cols] + jnp.sum(p, axis=0, keepdims=True)
        acc_ref[:, cols] = alpha * acc_ref[:, cols] + _bdot(vt_ref[c], p.astype(jnp.bfloat16))
        m_ref[:, cols] = m_new

    run(plain_tail)
    denom = l_ref[...]
    in_range = (denom >= ATTN_DENOM_MIN) & (denom <= ATTN_DENOM_MAX)
    n_bad = jnp.sum(jnp.where(in_range, 0.0, 1.0))

    @pl.when(n_bad > 0.0)
    def _():
        m_ref[...] = jnp.full_like(m_ref, -jnp.inf)
        run(online_tail)

    o_ref[...] = (acc_ref[...] / l_ref[...]).T.astype(o_ref.dtype)


def _attn(qt, kf, vt, tq, unroll):
    b, hh, _, s = qt.shape
    n_chunks, _, tk = vt.shape[2:]
    return pl.pallas_call(
        functools.partial(_attn_kernel, unroll=unroll),
        grid=(b, hh, s // tq),
        in_specs=[pl.BlockSpec((None, None, MLA_QK, tq), lambda i, h, q: (i, h, 0, q)),
                  pl.BlockSpec((None, None, s, MLA_QK), lambda i, h, q: (i, h, 0, 0)),
                  pl.BlockSpec((None, None, n_chunks, MLA_V, tk),
                               lambda i, h, q: (i, h, 0, 0, 0))],
        out_specs=pl.BlockSpec((None, tq, MLA_V), lambda i, h, q: (i, q, h)),
        out_shape=jax.ShapeDtypeStruct((b, s, MLA_V_W), jnp.bfloat16),
        scratch_shapes=[pltpu.VMEM((1, tq), jnp.float32),
                        pltpu.VMEM((1, tq), jnp.float32),
                        pltpu.VMEM((MLA_V, tq), jnp.float32)],
        compiler_params=_cparams(("parallel", "parallel", "parallel")),
        name="attn",
    )(qt, kf, vt)


def _split_bf16(x, pieces):
    out = []
    for _ in range(pieces):
        term = x.astype(jnp.bfloat16)
        out.append(term)
        x = x - term.astype(jnp.float32)
    return out


def _gla_cumsum(g, reverse, pieces):
    c = GLA_KERNEL_CHUNK
    row = lax.broadcasted_iota(jnp.int32, (c, c), 0)
    col = lax.broadcasted_iota(jnp.int32, (c, c), 1)
    tri = ((col >= row) if reverse else (col <= row)).astype(jnp.bfloat16)
    return _bdot(jnp.concatenate([tri] * pieces, axis=1),
                 jnp.concatenate(_split_bf16(g, pieces), axis=0))


def _gla_scores_factorised(q, k, b, keep):
    c = GLA_KERNEL_CHUNK
    d = b - b[c // 2:c // 2 + 1, :]
    q_in = (q * jnp.exp2(jnp.minimum(d, GLA_EXP2_CLAMP))).astype(jnp.bfloat16)
    k_in = (k * jnp.exp2(jnp.minimum(-d, GLA_EXP2_CLAMP))).astype(jnp.bfloat16)
    zeros = jnp.zeros((c, GLA_DK), jnp.bfloat16)
    keep2 = jnp.concatenate([keep, keep], axis=1)
    scores = []
    for hp in range(GLA_HEADS // 2):
        k0 = k_in[:, (2 * hp) * GLA_DK:(2 * hp + 1) * GLA_DK]
        k1 = k_in[:, (2 * hp + 1) * GLA_DK:(2 * hp + 2) * GLA_DK]
        k_diag = jnp.concatenate([jnp.concatenate([k0, zeros], axis=1),
                                  jnp.concatenate([zeros, k1], axis=1)], axis=0)
        a2 = _bdot_nt(q_in[:, 2 * hp * GLA_DK:(2 * hp + 2) * GLA_DK], k_diag)
        a2 = jnp.where(keep2, a2, 0.0).astype(jnp.bfloat16)
        scores += [a2[:, :c], a2[:, c:]]
    return scores


def _gla_scores_direct(q, k, b, keep, b_scr, k_scr):
    c = GLA_KERNEL_CHUNK
    b_scr[...] = b
    k_scr[...] = k
    col = lax.broadcasted_iota(jnp.int32, (c, c), 1)

    def key_row(s, accs):
        e = jnp.exp2(jnp.minimum(b - b_scr[pl.ds(s, 1), :], 0.0)) * (q * k_scr[pl.ds(s, 1), :])
        return tuple(
            jnp.where(col == s,
                      jnp.sum(e[:, h * GLA_DK:(h + 1) * GLA_DK], axis=1, keepdims=True), acc)
            for h, acc in enumerate(accs))

    accs = lax.fori_loop(0, c, key_row,
                         tuple(jnp.zeros((c, c), jnp.float32) for _ in range(GLA_HEADS)))
    return [jnp.where(keep, a, 0.0).astype(jnp.bfloat16) for a in accs]


def _gla_chunk_local(qkv_ref, b, rows, reverse, direct_scratch):
    c = GLA_KERNEL_CHUNK
    row = lax.broadcasted_iota(jnp.int32, (c, c), 0)
    col = lax.broadcasted_iota(jnp.int32, (c, c), 1)
    keep = (col > row) if reverse else (col <= row)
    last = 0 if reverse else c - 1
    b_last = b[last:last + 1, :]
    q = qkv_ref[rows, 0:GLA_QK_W].astype(jnp.float32)
    k = qkv_ref[rows, GLA_QK_W:2 * GLA_QK_W].astype(jnp.float32)
    q_st = (q * jnp.exp2(b)).astype(jnp.bfloat16)
    k_st = (k * jnp.exp2(b_last - b)).astype(jnp.bfloat16)
    e_last = jnp.exp2(b_last)
    if direct_scratch is None:
        scores = _gla_scores_factorised(q, k, b, keep)
    else:
        scores = _gla_scores_direct(q, k, b, keep, *direct_scratch)
    heads = []
    for h in range(GLA_HEADS):
        ks = slice(h * GLA_DK, (h + 1) * GLA_DK)
        v = qkv_ref[rows, 2 * GLA_QK_W + h * GLA_DV:2 * GLA_QK_W + (h + 1) * GLA_DV]
        upd = _bdot_tn(k_st[:, ks], v)
        decay_col = jnp.broadcast_to(e_last[:, ks], (GLA_DK, GLA_DK)).T
        decay = jnp.concatenate([decay_col] * (GLA_DV // GLA_DK), axis=1)
        heads.append((jnp.concatenate([scores[h], q_st[:, ks]], axis=1), v, upd, decay))
    return heads


def _gla_chunk_state(heads, rows, o_ref, state_ref):
    for h, (lhs, v, upd, decay) in enumerate(heads):
        st = state_ref[h]
        o = _bdot(lhs, jnp.concatenate([v, st.astype(jnp.bfloat16)], axis=0))
        o_ref[rows, h * GLA_DV:(h + 1) * GLA_DV] = o.astype(o_ref.dtype)
        state_ref[h] = st * decay + upd


def _gla_block(qkv_f_ref, g_f_ref, qkv_b_ref, g_b_ref, of_ref, ob_ref, sf_ref, sb_ref,
               direct_scratch):
    c = GLA_KERNEL_CHUNK
    n_chunks = qkv_f_ref.shape[0] // c
    work = []
    for ci in range(n_chunks):
        rows_f = slice(ci * c, (ci + 1) * c)
        rows_b = slice((n_chunks - 1 - ci) * c, (n_chunks - ci) * c)
        work.append((qkv_f_ref, g_f_ref, rows_f, False, of_ref, sf_ref))
        work.append((qkv_b_ref, g_b_ref, rows_b, True, ob_ref, sb_ref))
    pieces = 2 if direct_scratch is None else 3
    sums = [_gla_cumsum(g_ref[rows, :], rev, pieces) for (_, g_ref, rows, rev, _, _) in work]
    local = lambda i: _gla_chunk_local(work[i][0], sums[i], work[i][2], work[i][3], direct_scratch)
    pending = [local(i) for i in range(min(GLA_LOOKAHEAD, len(work)))]
    for i, (_, _, rows, _, o_ref, state_ref) in enumerate(work):
        heads = pending.pop(0)
        if i + GLA_LOOKAHEAD < len(work):
            pending.append(local(i + GLA_LOOKAHEAD))
        _gla_chunk_state(heads, rows, o_ref, state_ref)


def _gla_kernel(qkv_f_ref, g_f_ref, gmin_f_ref, qkv_b_ref, g_b_ref, gmin_b_ref,
                of_ref, ob_ref, sf_ref, sb_ref, b_scr, k_scr):
    @pl.when(pl.program_id(1) == 0)
    def _():
        sf_ref[...] = jnp.zeros_like(sf_ref)
        sb_ref[...] = jnp.zeros_like(sb_ref)

    refs = (qkv_f_ref, g_f_ref, qkv_b_ref, g_b_ref, of_ref, ob_ref, sf_ref, sb_ref)
    g_min = jnp.min(jnp.minimum(gmin_f_ref[...], gmin_b_ref[...]))
    steep = g_min * (GLA_KERNEL_CHUNK // 2) < -GLA_EXP2_CLAMP
    lax.cond(steep,
             lambda: _gla_block(*refs, (b_scr, k_scr)),
             lambda: _gla_block(*refs, None))


def _gla(qkv3, gates3, gmin4, tb):
    b, s, _ = qkv3.shape
    nb = s // tb
    assert gmin4.shape[:2] == (b, nb)
    fwd = lambda i, j: (i, j, 0)
    bwd = lambda i, j: (i, nb - 1 - j, 0)
    gmin_spec = lambda index_map: pl.BlockSpec((None, None, 8, V7X_LANES), index_map)
    return pl.pallas_call(
        _gla_kernel,
        grid=(b, nb),
        in_specs=[pl.BlockSpec((None, tb, QKV_W), fwd),
                  pl.BlockSpec((None, tb, GLA_QK_W), fwd),
                  gmin_spec(lambda i, j: (i, j, 0, 0)),
                  pl.BlockSpec((None, tb, QKV_W), bwd),
                  pl.BlockSpec((None, tb, GLA_QK_W), lambda i, j: (i, nb - 1 - j, 1)),
                  gmin_spec(lambda i, j: (i, nb - 1 - j, 0, 0))],
        out_specs=[pl.BlockSpec((None, tb, GLA_V_W), fwd),
                   pl.BlockSpec((None, tb, GLA_V_W), bwd)],
        out_shape=[jax.ShapeDtypeStruct((b, s, GLA_V_W), jnp.bfloat16),
                   jax.ShapeDtypeStruct((b, s, GLA_V_W), jnp.bfloat16)],
        scratch_shapes=[pltpu.VMEM((GLA_HEADS, GLA_DK, GLA_DV), jnp.float32),
                        pltpu.VMEM((GLA_HEADS, GLA_DK, GLA_DV), jnp.float32),
                        pltpu.VMEM((GLA_KERNEL_CHUNK, GLA_QK_W), jnp.float32),
                        pltpu.VMEM((GLA_KERNEL_CHUNK, GLA_QK_W), jnp.float32)],
        compiler_params=_cparams(("parallel", "arbitrary")),
        name="gla",
    )(qkv3, gates3, gmin4, qkv3, gates3, gmin4)


def _final_kernel(x_ref, of_ref, ob_ref, om_ref, act_ref, gg_ref, pg_ref, wa_ref, wb_ref, wo_ref,
                  y_ref):
    o_sum = of_ref[...].astype(jnp.float32) + ob_ref[...].astype(jnp.float32)
    gg = gg_ref[...]
    o_a = jnp.concatenate(
        [_rms(o_sum[:, h * GLA_DV:(h + 1) * GLA_DV], gg) for h in range(GLA_HEADS)], axis=1)
    y_a = (o_a * act_ref[:, 0:D_MODEL].astype(jnp.float32)).astype(jnp.bfloat16)
    y_b = om_ref[...] * act_ref[:, D_MODEL:2 * D_MODEL]
    merged = (act_ref[:, 2 * D_MODEL:3 * D_MODEL].astype(jnp.float32) * _bdot(y_a, wa_ref[...])
              + act_ref[:, 3 * D_MODEL:].astype(jnp.float32) * _bdot(y_b, wb_ref[...]))
    out = _bdot(merged.astype(jnp.bfloat16), wo_ref[...])
    y_ref[...] = x_ref[...] + _rms(out, pg_ref[...])


def _final(x2d, o_f, o_b, o_m, act, gg, pg, wa, wb, wo, ts):
    t = x2d.shape[0]
    tok = lambda w: pl.BlockSpec((ts, w), lambda i: (i, 0))
    return pl.pallas_call(
        _final_kernel,
        grid=(t // ts,),
        in_specs=[tok(D_MODEL), tok(GLA_V_W), tok(GLA_V_W), tok(MLA_V_W), tok(ACT_W),
                  _const_spec((1, GLA_DV)), _const_spec((1, D_MODEL)),
                  _const_spec((GLA_V_W, D_MODEL)), _const_spec((MLA_V_W, D_MODEL)),
                  _const_spec((D_MODEL, D_MODEL))],
        out_specs=tok(D_MODEL),
        out_shape=jax.ShapeDtypeStruct((t, D_MODEL), jnp.float32),
        compiler_params=_cparams(("parallel",)),
        name="final",
    )(x2d, o_f, o_b, o_m, act, gg, pg, wa, wb, wo)


def _rotate_half_cols(w):
    half = MLA_ROPE // 2
    w3 = w.reshape(w.shape[0], -1, MLA_ROPE)
    return jnp.concatenate([-w3[..., half:], w3[..., :half]], axis=-1).reshape(w.shape)


def _pad_cols(w, width):
    return jnp.pad(w, ((0, 0), (0, width - w.shape[1])))


def _prepare_weights(pre_g, w_in, wg2_f, bg2_f, wg2_b, bg2_b, gla_g, q_g, w_uq, kv_g, w_ukv,
                     w_a, w_b, w_o, post_g):
    bf = jnp.bfloat16
    sizes = (GLA_QK_W, GLA_QK_W, GLA_V_W, GLA_GATE_RANK, GLA_GATE_RANK, GLA_V_W,
             MLA_Q_RANK, MLA_KV_RANK, MLA_ROPE, MLA_V_W, D_MODEL, D_MODEL)
    offs = [0]
    for sz in sizes:
        offs.append(offs[-1] + sz)
    (w_q, w_k, w_v, w_grf, w_grb, w_za, w_cq, w_ckv, w_kr, w_zb, w_ga, w_gb) = [
        w_in[:, offs[i]:offs[i + 1]] for i in range(len(sizes))]
    wqkv = jnp.concatenate([w_q, w_k, w_v], axis=1).astype(bf)
    wlat = jnp.concatenate([
        w_cq, w_ckv, _pad_cols(w_kr, V7X_LANES), _pad_cols(_rotate_half_cols(w_kr), V7X_LANES),
        _pad_cols(jnp.concatenate([w_grf, w_grb], axis=1), V7X_LANES)], axis=1).astype(bf)
    wact = jnp.concatenate([w_za, w_zb, w_ga, w_gb], axis=1).astype(bf)

    wg = jnp.zeros((V7X_LANES, GATES_W), jnp.float32)
    wg = wg.at[0:GLA_GATE_RANK, :GLA_QK_W].set(wg2_f)
    wg = wg.at[GLA_GATE_RANK:2 * GLA_GATE_RANK, GLA_QK_W:].set(wg2_b).astype(bf)
    bg = jnp.concatenate([bg2_f, bg2_b])[None, :]

    uq = w_uq.reshape(MLA_Q_RANK, MLA_HEADS, MLA_QK)
    wqn = uq[:, :, :MLA_NOPE].reshape(MLA_Q_RANK, MLA_HEADS * MLA_NOPE)
    wqr = uq[:, :, MLA_NOPE:].reshape(MLA_Q_RANK, MLA_HEADS * MLA_ROPE)
    ukv = w_ukv.reshape(MLA_KV_RANK, MLA_HEADS, MLA_NOPE + MLA_V)
    wkn = ukv[:, :, :MLA_NOPE].reshape(MLA_KV_RANK, MLA_HEADS * MLA_NOPE)
    wv = ukv[:, :, MLA_NOPE:].reshape(MLA_KV_RANK, MLA_HEADS * MLA_V)
    return dict(
        pre_g=pre_g[None, :], wqkv=wqkv, wlat=wlat, wact=wact,
        wg=wg, bg=bg,
        gla_g=gla_g[None, :], q_g=q_g[None, :], kv_g=kv_g[None, :],
        wqn=wqn.astype(bf), wqr=wqr.astype(bf), wqrot=_rotate_half_cols(wqr).astype(bf),
        wkn=wkn.astype(bf), wv=wv.astype(bf),
        w_a=w_a.astype(bf), w_b=w_b.astype(bf), w_o=w_o.astype(bf), post_g=post_g[None, :])


def _rope_tables(s):
    half = MLA_ROPE // 2
    freqs = ROPE_THETA ** (-jnp.arange(half, dtype=jnp.float32) / half)
    ang = jnp.arange(s).astype(jnp.float32)[:, None] * freqs[None, :]
    cos = jnp.cos(ang)
    sin = jnp.sin(ang)
    return jnp.concatenate([cos, cos], axis=1), jnp.concatenate([sin, sin], axis=1)


def _tiles(b, s):
    tk = min(512, s)
    return dict(ts=min(512, s), ts_up=min(1024, s), tk=tk, tq=min(2048, s),
                attn_unroll=min(8, s // tk))


def _layer(x, w):
    b, s, _ = x.shape
    t = b * s
    tl = _tiles(b, s)
    x2d = x.reshape(t, D_MODEL)
    qkv, lat, act, gates, gmin = _proj(x2d, w["pre_g"], w["wqkv"], w["wlat"], w["wact"], w["wg"],
                                       w["bg"], tl["ts"])
    lat3 = lat.reshape(b, s, LAT_GR)
    cos, sin = _rope_tables(s)
    qt, kf, vt = _mla_up(lat3, cos, sin, w["q_g"], w["kv_g"], w["wqn"], w["wqr"], w["wqrot"],
                         w["wkn"], w["wv"], tl["ts_up"], tl["tk"])
    o_m = _attn(qt, kf, vt, tl["tq"], tl["attn_unroll"])
    o_f, o_b = _gla(qkv.reshape(b, s, QKV_W), gates.reshape(b, s, GATES_W),
                    gmin.reshape(b, s // tl["ts"], 8, V7X_LANES), tl["ts"])
    y = _final(x2d, o_f.reshape(t, GLA_V_W), o_b.reshape(t, GLA_V_W), o_m.reshape(t, MLA_V_W),
               act, w["gla_g"], w["post_g"], w["w_a"], w["w_b"], w["w_o"], tl["ts"])
    return y.reshape(b, s, D_MODEL)


def kernel(x_prompt, x_sample, pre_norm_g, w_in, gla_wg2_fwd, gla_bg2_fwd, gla_wg2_bwd, gla_bg2_bwd,
           gla_out_norm_g, mla_q_norm_g, mla_w_uq, mla_kv_norm_g, mla_w_ukv, w_branch_a, w_branch_b,
           w_out, post_norm_g):
    depth = pre_norm_g.shape[0]
    y_prompt, y_sample = x_prompt, x_sample
    for l in range(depth):
        w = _prepare_weights(pre_norm_g[l], w_in[l], gla_wg2_fwd[l], gla_bg2_fwd[l], gla_wg2_bwd[l],
                             gla_bg2_bwd[l], gla_out_norm_g[l], mla_q_norm_g[l], mla_w_uq[l],
                             mla_kv_norm_g[l], mla_w_ukv[l], w_branch_a[l], w_branch_b[l],
                             w_out[l], post_norm_g[l])
        y_prompt = _layer(y_prompt, w)
        y_sample = _layer(y_sample, w)
    return (y_prompt, y_sample)
```

```python
import functools
import math

import jax
import jax.numpy as jnp
from jax import lax
from jax.experimental import pallas as pl
from jax.experimental.pallas import tpu as pltpu

D_MODEL = 1024
GLA_HEADS = 4
GLA_DK = 128
GLA_DV = 256
GLA_GATE_RANK = 16
GLA_GATE_NORM = 16.0
MLA_HEADS = 8
MLA_Q_RANK = 256
MLA_KV_RANK = 128
MLA_NOPE = 128
MLA_ROPE = 64
MLA_V = 128
ROPE_THETA = 10000.0
EPS = 1e-6

GLA_QK_W = GLA_HEADS * GLA_DK
GLA_V_W = GLA_HEADS * GLA_DV
MLA_V_W = MLA_HEADS * MLA_V
MLA_QK = MLA_NOPE + MLA_ROPE

V7X_LANES = 128
V7X_VMEM_LIMIT_BYTES = 56 * 1024 * 1024

QKV_W = 2 * GLA_QK_W + GLA_V_W
LAT_CQ = 0
LAT_CKV = MLA_Q_RANK
LAT_KR = LAT_CKV + MLA_KV_RANK
LAT_KROT = LAT_KR + V7X_LANES
LAT_GR = LAT_KROT + V7X_LANES
LAT_W = LAT_GR + V7X_LANES
ACT_W = 4 * D_MODEL
GATES_W = 2 * GLA_QK_W

GLA_KERNEL_CHUNK = 128
GLA_LOOKAHEAD = 4
GLA_EXP2_CLAMP = 115.0
LOG2E = 1.4426950408889634
ATTN_QSUB = 256
ATTN_LOOKAHEAD = 3
ATTN_DENOM_MIN = 2.0 ** -60
ATTN_DENOM_MAX = 2.0 ** 60


def _cparams(semantics):
    return pltpu.CompilerParams(dimension_semantics=semantics,
                                vmem_limit_bytes=V7X_VMEM_LIMIT_BYTES)


def _const_spec(shape):
    return pl.BlockSpec(shape, lambda *_: (0,) * len(shape))


def _rms(x, g):
    return x * lax.rsqrt(jnp.mean(x * x, axis=-1, keepdims=True) + EPS) * g


def _sigmoid(x):
    return 1.0 / (1.0 + jnp.exp(-x))


def _bdot(a, b):
    return jnp.dot(a, b, preferred_element_type=jnp.float32)


def _bdot_nt(a, b):
    return lax.dot_general(a, b, (((1,), (1,)), ((), ())), preferred_element_type=jnp.float32)


def _bdot_tn(a, b):
    return lax.dot_general(a, b, (((0,), (0,)), ((), ())), preferred_element_type=jnp.float32)


def _proj_kernel(x_ref, g_ref, wqkv_ref, wlat_ref, wact_ref, wg_ref, bg_ref,
                 qkv_ref, lat_ref, act_ref, gates_ref, gmin_ref):
    half = x_ref.shape[0] // 2
    gate_mins = []
    for r in (slice(0, half), slice(half, 2 * half)):
        h = _rms(x_ref[r, :], g_ref[...]).astype(jnp.bfloat16)
        lat = _bdot(h, wlat_ref[...])
        xg = _bdot(lat[:, LAT_GR:].astype(jnp.bfloat16), wg_ref[...]) + bg_ref[...]
        qkv = _bdot(h, wqkv_ref[...])
        lat_ref[r, :] = lat[:, :LAT_GR]
        gates = ((jnp.minimum(xg, 0.0) - jnp.log(1.0 + jnp.exp(-jnp.abs(xg))))
                 * (LOG2E / GLA_GATE_NORM))
        gates_ref[r, :] = gates
        by_sublane = jnp.min(gates.reshape(half // 8, 8, GATES_W), axis=0)
        gate_mins += [by_sublane[:, i * V7X_LANES:(i + 1) * V7X_LANES]
                      for i in range(GATES_W // V7X_LANES)]
        z = _bdot(h, wact_ref[:, :2 * D_MODEL])
        qkv_ref[r, :GLA_QK_W] = (qkv[:, :GLA_QK_W] * (GLA_DK ** -0.5)).astype(jnp.bfloat16)
        qkv_ref[r, GLA_QK_W:] = qkv[:, GLA_QK_W:].astype(jnp.bfloat16)
        gate = _bdot(h, wact_ref[:, 2 * D_MODEL:])
        act_ref[r, :2 * D_MODEL] = (z * _sigmoid(z)).astype(jnp.bfloat16)
        act_ref[r, 2 * D_MODEL:] = _sigmoid(gate).astype(jnp.bfloat16)
    gmin_ref[...] = functools.reduce(jnp.minimum, gate_mins)


def _proj(x2d, pre_g, wqkv, wlat, wact, wg, bg, ts):
    t = x2d.shape[0]
    return pl.pallas_call(
        _proj_kernel,
        grid=(t // ts,),
        in_specs=[pl.BlockSpec((ts, D_MODEL), lambda i: (i, 0)),
                  _const_spec((1, D_MODEL)),
                  _const_spec((D_MODEL, QKV_W)),
                  _const_spec((D_MODEL, LAT_W)),
                  _const_spec((D_MODEL, ACT_W)),
                  _const_spec((V7X_LANES, GATES_W)),
                  _const_spec((1, GATES_W))],
        out_specs=[pl.BlockSpec((ts, QKV_W), lambda i: (i, 0)),
                   pl.BlockSpec((ts, LAT_GR), lambda i: (i, 0)),
                   pl.BlockSpec((ts, ACT_W), lambda i: (i, 0)),
                   pl.BlockSpec((ts, GATES_W), lambda i: (i, 0)),
                   pl.BlockSpec((None, 8, V7X_LANES), lambda i: (i, 0, 0))],
        out_shape=[jax.ShapeDtypeStruct((t, QKV_W), jnp.bfloat16),
                   jax.ShapeDtypeStruct((t, LAT_GR), jnp.float32),
                   jax.ShapeDtypeStruct((t, ACT_W), jnp.bfloat16),
                   jax.ShapeDtypeStruct((t, GATES_W), jnp.float32),
                   jax.ShapeDtypeStruct((t // ts, 8, V7X_LANES), jnp.float32)],
        compiler_params=_cparams(("parallel",)),
        name="proj",
    )(x2d, pre_g, wqkv, wlat, wact, wg, bg)


def _mla_up_kernel(lat_ref, cos_ref, sin_ref, qg_ref, kvg_ref, wqn_ref, wqr_ref, wqrot_ref,
                   wkn_ref, wv_ref, qt_ref, kf_ref, vt_ref):
    lat = lat_ref[...]
    cos = cos_ref[...]
    sin = sin_ref[...]
    c_q = _rms(lat[:, LAT_CQ:LAT_CQ + MLA_Q_RANK], qg_ref[...]).astype(jnp.bfloat16)
    c_kv = _rms(lat[:, LAT_CKV:LAT_CKV + MLA_KV_RANK], kvg_ref[...]).astype(jnp.bfloat16)

    q_scale = (MLA_QK ** -0.5) * LOG2E
    qn_t = (_bdot(c_q, wqn_ref[...]) * q_scale).T
    cos_h = jnp.concatenate([cos, cos], axis=1)
    sin_h = jnp.concatenate([sin, sin], axis=1)
    cos_q = jnp.concatenate([cos_h] * (MLA_HEADS // 2), axis=1)
    sin_q = jnp.concatenate([sin_h] * (MLA_HEADS // 2), axis=1)
    q_pe = _bdot(c_q, wqr_ref[...]) * cos_q + _bdot(c_q, wqrot_ref[...]) * sin_q
    qpe_t = (q_pe * q_scale).T

    k_pe = (lat[:, LAT_KR:LAT_KR + MLA_ROPE] * cos
            + lat[:, LAT_KROT:LAT_KROT + MLA_ROPE] * sin).astype(jnp.bfloat16)
    kn = _bdot(c_kv, wkn_ref[...]).astype(jnp.bfloat16)
    v_t = _bdot(c_kv, wv_ref[...]).T

    n_kc, tk = vt_ref.shape[1], vt_ref.shape[3]
    for h in range(MLA_HEADS):
        qt_ref[h, :MLA_NOPE, :] = qn_t[h * MLA_NOPE:(h + 1) * MLA_NOPE, :].astype(jnp.bfloat16)
        qt_ref[h, MLA_NOPE:, :] = qpe_t[h * MLA_ROPE:(h + 1) * MLA_ROPE, :].astype(jnp.bfloat16)
        kf_ref[h, :, :MLA_NOPE] = kn[:, h * MLA_NOPE:(h + 1) * MLA_NOPE]
        kf_ref[h, :, MLA_NOPE:] = k_pe
        for u in range(n_kc):
            vt_ref[h, u] = v_t[h * MLA_V:(h + 1) * MLA_V, u * tk:(u + 1) * tk].astype(jnp.bfloat16)


def _mla_up(lat3, cos, sin, qg, kvg, wqn, wqr, wqrot, wkn, wv, ts, tk):
    b, s, _ = lat3.shape
    hh = MLA_HEADS
    return pl.pallas_call(
        _mla_up_kernel,
        grid=(b, s // ts),
        in_specs=[pl.BlockSpec((None, ts, LAT_GR), lambda i, j: (i, j, 0)),
                  pl.BlockSpec((ts, MLA_ROPE), lambda i, j: (j, 0)),
                  pl.BlockSpec((ts, MLA_ROPE), lambda i, j: (j, 0)),
                  _const_spec((1, MLA_Q_RANK)),
                  _const_spec((1, MLA_KV_RANK)),
                  _const_spec((MLA_Q_RANK, hh * MLA_NOPE)),
                  _const_spec((MLA_Q_RANK, hh * MLA_ROPE)),
                  _const_spec((MLA_Q_RANK, hh * MLA_ROPE)),
                  _const_spec((MLA_KV_RANK, hh * MLA_NOPE)),
                  _const_spec((MLA_KV_RANK, hh * MLA_V))],
        out_specs=[pl.BlockSpec((None, hh, MLA_QK, ts), lambda i, j: (i, 0, 0, j)),
                   pl.BlockSpec((None, hh, ts, MLA_QK), lambda i, j: (i, 0, j, 0)),
                   pl.BlockSpec((None, hh, ts // tk, MLA_V, tk), lambda i, j: (i, 0, j, 0, 0))],
        out_shape=[jax.ShapeDtypeStruct((b, hh, MLA_QK, s), jnp.bfloat16),
                   jax.ShapeDtypeStruct((b, hh, s, MLA_QK), jnp.bfloat16),
                   jax.ShapeDtypeStruct((b, hh, s // tk, MLA_V, tk), jnp.bfloat16)],
        compiler_params=_cparams(("parallel", "parallel")),
        name="mla_up",
    )(lat3, cos, sin, qg, kvg, wqn, wqr, wqrot, wkn, wv)


def _attn_kernel(qt_ref, k_ref, vt_ref, szb_ref, o_ref, m_ref, l_ref, acc_ref, *, unroll):
    n_chunks, _, tk = vt_ref.shape
    n_sub = qt_ref.shape[1] // ATTN_QSUB

    def scores(c, j):
        rows = pl.ds(pl.multiple_of(c * tk, tk), tk)
        return _bdot(k_ref[rows, :], qt_ref[:, j * ATTN_QSUB:(j + 1) * ATTN_QSUB])

    def run(chain_tail):
        acc_ref[...] = jnp.zeros_like(acc_ref)
        l_ref[...] = jnp.zeros_like(l_ref)

        def group(g, carry):
            chains = [(g * unroll + u, j) for u in range(unroll) for j in range(n_sub)]
            pending = [scores(*ch) for ch in chains[:ATTN_LOOKAHEAD]]
            for idx, (c, j) in enumerate(chains):
                s = pending.pop(0)
                if idx + ATTN_LOOKAHEAD < len(chains):
                    pending.append(scores(*chains[idx + ATTN_LOOKAHEAD]))
                chain_tail(s, c, slice(j * ATTN_QSUB, (j + 1) * ATTN_QSUB))
            return carry

        n_groups = n_chunks // unroll
        if n_groups == 1:
            group(0, 0)
        else:
            lax.fori_loop(0, n_groups, group, 0)

    def plain_tail(s, c, cols):
        p = jnp.exp2(s)
        l_ref[:, cols] = l_ref[:, cols] + jnp.sum(p, axis=0, keepdims=True)
        acc_ref[:, cols] = acc_ref[:, cols] + _bdot(vt_ref[c], p.astype(jnp.bfloat16))

    def online_tail(s, c, cols):
        m_old = m_ref[:, cols]
        m_new = jnp.maximum(m_old, jnp.max(s, axis=0, keepdims=True))
        alpha = jnp.exp2(m_old - m_new)
        p = jnp.exp2(s - m_new)
        l_ref[:, cols] = alpha * l_ref[:, cols] + jnp.sum(p, axis=0, keepdims=True)
        acc_ref[:, cols] = alpha * acc_ref[:, cols] + _bdot(vt_ref[c], p.astype(jnp.bfloat16))
        m_ref[:, cols] = m_new

    run(plain_tail)
    denom = l_ref[...]
    in_range = (denom >= ATTN_DENOM_MIN) & (denom <= ATTN_DENOM_MAX)
    n_bad = jnp.sum(jnp.where(in_range, 0.0, 1.0))

    @pl.when(n_bad > 0.0)
    def _():
        m_ref[...] = jnp.full_like(m_ref, -jnp.inf)
        run(online_tail)

    attn = (acc_ref[...] / l_ref[...]).T
    o_ref[...] = (attn * szb_ref[...].astype(jnp.float32)).astype(o_ref.dtype)


def _attn(qt, kf, vt, act3, tq, unroll):
    b, hh, _, s = qt.shape
    n_chunks, _, tk = vt.shape[2:]
    szb_block0 = D_MODEL // MLA_V
    return pl.pallas_call(
        functools.partial(_attn_kernel, unroll=unroll),
        grid=(b, hh, s // tq),
        in_specs=[pl.BlockSpec((None, None, MLA_QK, tq), lambda i, h, q: (i, h, 0, q)),
                  pl.BlockSpec((None, None, s, MLA_QK), lambda i, h, q: (i, h, 0, 0)),
                  pl.BlockSpec((None, None, n_chunks, MLA_V, tk),
                               lambda i, h, q: (i, h, 0, 0, 0)),
                  pl.BlockSpec((None, tq, MLA_V), lambda i, h, q: (i, q, szb_block0 + h))],
        out_specs=pl.BlockSpec((None, tq, MLA_V), lambda i, h, q: (i, q, h)),
        out_shape=jax.ShapeDtypeStruct((b, s, MLA_V_W), jnp.bfloat16),
        scratch_shapes=[pltpu.VMEM((1, tq), jnp.float32),
                        pltpu.VMEM((1, tq), jnp.float32),
                        pltpu.VMEM((MLA_V, tq), jnp.float32)],
        compiler_params=_cparams(("parallel", "parallel", "parallel")),
        name="attn",
    )(qt, kf, vt, act3)


def _split_bf16(x, pieces):
    out = []
    for _ in range(pieces):
        term = x.astype(jnp.bfloat16)
        out.append(term)
        x = x - term.astype(jnp.float32)
    return out


def _gla_cumsum(g, reverse, pieces):
    c = GLA_KERNEL_CHUNK
    row = lax.broadcasted_iota(jnp.int32, (c, c), 0)
    col = lax.broadcasted_iota(jnp.int32, (c, c), 1)
    tri = ((col >= row) if reverse else (col <= row)).astype(jnp.bfloat16)
    return _bdot(jnp.concatenate([tri] * pieces, axis=1),
                 jnp.concatenate(_split_bf16(g, pieces), axis=0))


def _gla_scores_factorised(q, k, b, keep):
    c = GLA_KERNEL_CHUNK
    d = b - b[c // 2:c // 2 + 1, :]
    q_in = (q * jnp.exp2(jnp.minimum(d, GLA_EXP2_CLAMP))).astype(jnp.bfloat16)
    k_in = (k * jnp.exp2(jnp.minimum(-d, GLA_EXP2_CLAMP))).astype(jnp.bfloat16)
    zeros = jnp.zeros((c, GLA_DK), jnp.bfloat16)
    keep2 = jnp.concatenate([keep, keep], axis=1)
    scores = []
    for hp in range(GLA_HEADS // 2):
        k0 = k_in[:, (2 * hp) * GLA_DK:(2 * hp + 1) * GLA_DK]
        k1 = k_in[:, (2 * hp + 1) * GLA_DK:(2 * hp + 2) * GLA_DK]
        k_diag = jnp.concatenate([jnp.concatenate([k0, zeros], axis=1),
                                  jnp.concatenate([zeros, k1], axis=1)], axis=0)
        a2 = _bdot_nt(q_in[:, 2 * hp * GLA_DK:(2 * hp + 2) * GLA_DK], k_diag)
        a2 = jnp.where(keep2, a2, 0.0).astype(jnp.bfloat16)
        scores += [a2[:, :c], a2[:, c:]]
    return scores


def _gla_scores_direct(q, k, b, keep, b_scr, k_scr):
    c = GLA_KERNEL_CHUNK
    b_scr[...] = b
    k_scr[...] = k
    col = lax.broadcasted_iota(jnp.int32, (c, c), 1)

    def key_row(s, accs):
        e = jnp.exp2(jnp.minimum(b - b_scr[pl.ds(s, 1), :], 0.0)) * (q * k_scr[pl.ds(s, 1), :])
        return tuple(
            jnp.where(col == s,
                      jnp.sum(e[:, h * GLA_DK:(h + 1) * GLA_DK], axis=1, keepdims=True), acc)
            for h, acc in enumerate(accs))

    accs = lax.fori_loop(0, c, key_row,
                         tuple(jnp.zeros((c, c), jnp.float32) for _ in range(GLA_HEADS)))
    return [jnp.where(keep, a, 0.0).astype(jnp.bfloat16) for a in accs]


def _gla_chunk_local(qkv_ref, b, rows, reverse, direct_scratch):
    c = GLA_KERNEL_CHUNK
    row = lax.broadcasted_iota(jnp.int32, (c, c), 0)
    col = lax.broadcasted_iota(jnp.int32, (c, c), 1)
    keep = (col > row) if reverse else (col <= row)
    last = 0 if reverse else c - 1
    b_last = b[last:last + 1, :]
    q = qkv_ref[rows, 0:GLA_QK_W].astype(jnp.float32)
    k = qkv_ref[rows, GLA_QK_W:2 * GLA_QK_W].astype(jnp.float32)
    q_st = (q * jnp.exp2(b)).astype(jnp.bfloat16)
    k_st = (k * jnp.exp2(b_last - b)).astype(jnp.bfloat16)
    e_last = jnp.exp2(b_last)
    if direct_scratch is None:
        scores = _gla_scores_factorised(q, k, b, keep)
    else:
        scores = _gla_scores_direct(q, k, b, keep, *direct_scratch)
    heads = []
    for h in range(GLA_HEADS):
        ks = slice(h * GLA_DK, (h + 1) * GLA_DK)
        v = qkv_ref[rows, 2 * GLA_QK_W + h * GLA_DV:2 * GLA_QK_W + (h + 1) * GLA_DV]
        upd = _bdot_tn(k_st[:, ks], v)
        decay_col = jnp.broadcast_to(e_last[:, ks], (GLA_DK, GLA_DK)).T
        decay = jnp.concatenate([decay_col] * (GLA_DV // GLA_DK), axis=1)
        heads.append((jnp.concatenate([scores[h], q_st[:, ks]], axis=1), v, upd, decay))
    return heads


def _gla_chunk_state(heads, rows, o_ref, state_ref):
    for h, (lhs, v, upd, decay) in enumerate(heads):
        st = state_ref[h]
        o = _bdot(lhs, jnp.concatenate([v, st.astype(jnp.bfloat16)], axis=0))
        o_ref[rows, h * GLA_DV:(h + 1) * GLA_DV] = o.astype(o_ref.dtype)
        state_ref[h] = st * decay + upd


def _gla_block(qkv_f_ref, g_f_ref, qkv_b_ref, g_b_ref, of_ref, ob_ref, sf_ref, sb_ref,
               direct_scratch):
    c = GLA_KERNEL_CHUNK
    n_chunks = qkv_f_ref.shape[0] // c
    work = []
    for ci in range(n_chunks):
        rows_f = slice(ci * c, (ci + 1) * c)
        rows_b = slice((n_chunks - 1 - ci) * c, (n_chunks - ci) * c)
        work.append((qkv_f_ref, g_f_ref, rows_f, False, of_ref, sf_ref))
        work.append((qkv_b_ref, g_b_ref, rows_b, True, ob_ref, sb_ref))
    pieces = 2 if direct_scratch is None else 3
    sums = [_gla_cumsum(g_ref[rows, :], rev, pieces) for (_, g_ref, rows, rev, _, _) in work]
    local = lambda i: _gla_chunk_local(work[i][0], sums[i], work[i][2], work[i][3], direct_scratch)
    pending = [local(i) for i in range(min(GLA_LOOKAHEAD, len(work)))]
    for i, (_, _, rows, _, o_ref, state_ref) in enumerate(work):
        heads = pending.pop(0)
        if i + GLA_LOOKAHEAD < len(work):
            pending.append(local(i + GLA_LOOKAHEAD))
        _gla_chunk_state(heads, rows, o_ref, state_ref)


def _gla_kernel(qkv_f_ref, g_f_ref, gmin_f_ref, qkv_b_ref, g_b_ref, gmin_b_ref,
                of_ref, ob_ref, sf_ref, sb_ref, b_scr, k_scr):
    @pl.when(pl.program_id(1) == 0)
    def _():
        sf_ref[...] = jnp.zeros_like(sf_ref)
        sb_ref[...] = jnp.zeros_like(sb_ref)

    refs = (qkv_f_ref, g_f_ref, qkv_b_ref, g_b_ref, of_ref, ob_ref, sf_ref, sb_ref)
    g_min = jnp.min(jnp.minimum(gmin_f_ref[...], gmin_b_ref[...]))
    steep = g_min * (GLA_KERNEL_CHUNK // 2) < -GLA_EXP2_CLAMP
    lax.cond(steep,
             lambda: _gla_block(*refs, (b_scr, k_scr)),
             lambda: _gla_block(*refs, None))


def _gla(qkv3, gates3, gmin4, tb):
    b, s, _ = qkv3.shape
    nb = s // tb
    assert gmin4.shape[:2] == (b, nb)
    fwd = lambda i, j: (i, j, 0)
    bwd = lambda i, j: (i, nb - 1 - j, 0)
    gmin_spec = lambda index_map: pl.BlockSpec((None, None, 8, V7X_LANES), index_map)
    return pl.pallas_call(
        _gla_kernel,
        grid=(b, nb),
        in_specs=[pl.BlockSpec((None, tb, QKV_W), fwd),
                  pl.BlockSpec((None, tb, GLA_QK_W), fwd),
                  gmin_spec(lambda i, j: (i, j, 0, 0)),
                  pl.BlockSpec((None, tb, QKV_W), bwd),
                  pl.BlockSpec((None, tb, GLA_QK_W), lambda i, j: (i, nb - 1 - j, 1)),
                  gmin_spec(lambda i, j: (i, nb - 1 - j, 0, 0))],
        out_specs=[pl.BlockSpec((None, tb, GLA_V_W), fwd),
                   pl.BlockSpec((None, tb, GLA_V_W), bwd)],
        out_shape=[jax.ShapeDtypeStruct((b, s, GLA_V_W), jnp.bfloat16),
                   jax.ShapeDtypeStruct((b, s, GLA_V_W), jnp.bfloat16)],
        scratch_shapes=[pltpu.VMEM((GLA_HEADS, GLA_DK, GLA_DV), jnp.float32),
                        pltpu.VMEM((GLA_HEADS, GLA_DK, GLA_DV), jnp.float32),
                        pltpu.VMEM((GLA_KERNEL_CHUNK, GLA_QK_W), jnp.float32),
                        pltpu.VMEM((GLA_KERNEL_CHUNK, GLA_QK_W), jnp.float32)],
        compiler_params=_cparams(("parallel", "arbitrary")),
        name="gla",
    )(qkv3, gates3, gmin4, qkv3, gates3, gmin4)


def _final_kernel(x_ref, of_ref, ob_ref, yb_ref, sza_ref, sga_ref, sgb_ref, gg_ref, pg_ref,
                  wa_ref, wb_ref, wo_ref, y_ref):
    o_sum = of_ref[...].astype(jnp.float32) + ob_ref[...].astype(jnp.float32)
    gg = gg_ref[...]
    o_a = jnp.concatenate(
        [_rms(o_sum[:, h * GLA_DV:(h + 1) * GLA_DV], gg) for h in range(GLA_HEADS)], axis=1)
    y_a = (o_a * sza_ref[...].astype(jnp.float32)).astype(jnp.bfloat16)
    merged = (sga_ref[...].astype(jnp.float32) * _bdot(y_a, wa_ref[...])
              + sgb_ref[...].astype(jnp.float32) * _bdot(yb_ref[...], wb_ref[...]))
    out = _bdot(merged.astype(jnp.bfloat16), wo_ref[...])
    y_ref[...] = x_ref[...] + _rms(out, pg_ref[...])


def _final(x2d, o_f, o_b, y_b, act, gg, pg, wa, wb, wo, ts):
    t = x2d.shape[0]
    tok = lambda w: pl.BlockSpec((ts, w), lambda i: (i, 0))
    act_cols = lambda blk: pl.BlockSpec((ts, D_MODEL), lambda i: (i, blk))
    return pl.pallas_call(
        _final_kernel,
        grid=(t // ts,),
        in_specs=[tok(D_MODEL), tok(GLA_V_W), tok(GLA_V_W), tok(MLA_V_W),
                  act_cols(0), act_cols(2), act_cols(3),
                  _const_spec((1, GLA_DV)), _const_spec((1, D_MODEL)),
                  _const_spec((GLA_V_W, D_MODEL)), _const_spec((MLA_V_W, D_MODEL)),
                  _const_spec((D_MODEL, D_MODEL))],
        out_specs=tok(D_MODEL),
        out_shape=jax.ShapeDtypeStruct((t, D_MODEL), jnp.float32),
        compiler_params=_cparams(("parallel",)),
        name="final",
    )(x2d, o_f, o_b, y_b, act, act, act, gg, pg, wa, wb, wo)


def _rotate_half_cols(w):
    half = MLA_ROPE // 2
    w3 = w.reshape(w.shape[0], -1, MLA_ROPE)
    return jnp.concatenate([-w3[..., half:], w3[..., :half]], axis=-1).reshape(w.shape)


def _pad_cols(w, width):
    return jnp.pad(w, ((0, 0), (0, width - w.shape[1])))


def _prepare_weights(pre_g, w_in, wg2_f, bg2_f, wg2_b, bg2_b, gla_g, q_g, w_uq, kv_g, w_ukv,
                     w_a, w_b, w_o, post_g):
    bf = jnp.bfloat16
    sizes = (GLA_QK_W, GLA_QK_W, GLA_V_W, GLA_GATE_RANK, GLA_GATE_RANK, GLA_V_W,
             MLA_Q_RANK, MLA_KV_RANK, MLA_ROPE, MLA_V_W, D_MODEL, D_MODEL)
    offs = [0]
    for sz in sizes:
        offs.append(offs[-1] + sz)
    (w_q, w_k, w_v, w_grf, w_grb, w_za, w_cq, w_ckv, w_kr, w_zb, w_ga, w_gb) = [
        w_in[:, offs[i]:offs[i + 1]] for i in range(len(sizes))]
    wqkv = jnp.concatenate([w_q, w_k, w_v], axis=1).astype(bf)
    wlat = jnp.concatenate([
        w_cq, w_ckv, _pad_cols(w_kr, V7X_LANES), _pad_cols(_rotate_half_cols(w_kr), V7X_LANES),
        _pad_cols(jnp.concatenate([w_grf, w_grb], axis=1), V7X_LANES)], axis=1).astype(bf)
    wact = jnp.concatenate([w_za, w_zb, w_ga, w_gb], axis=1).astype(bf)

    wg = jnp.zeros((V7X_LANES, GATES_W), jnp.float32)
    wg = wg.at[0:GLA_GATE_RANK, :GLA_QK_W].set(wg2_f)
    wg = wg.at[GLA_GATE_RANK:2 * GLA_GATE_RANK, GLA_QK_W:].set(wg2_b).astype(bf)
    bg = jnp.concatenate([bg2_f, bg2_b])[None, :]

    uq = w_uq.reshape(MLA_Q_RANK, MLA_HEADS, MLA_QK)
    wqn = uq[:, :, :MLA_NOPE].reshape(MLA_Q_RANK, MLA_HEADS * MLA_NOPE)
    wqr = uq[:, :, MLA_NOPE:].reshape(MLA_Q_RANK, MLA_HEADS * MLA_ROPE)
    ukv = w_ukv.reshape(MLA_KV_RANK, MLA_HEADS, MLA_NOPE + MLA_V)
    wkn = ukv[:, :, :MLA_NOPE].reshape(MLA_KV_RANK, MLA_HEADS * MLA_NOPE)
    wv = ukv[:, :, MLA_NOPE:].reshape(MLA_KV_RANK, MLA_HEADS * MLA_V)
    return dict(
        pre_g=pre_g[None, :], wqkv=wqkv, wlat=wlat, wact=wact,
        wg=wg, bg=bg,
        gla_g=gla_g[None, :], q_g=q_g[None, :], kv_g=kv_g[None, :],
        wqn=wqn.astype(bf), wqr=wqr.astype(bf), wqrot=_rotate_half_cols(wqr).astype(bf),
        wkn=wkn.astype(bf), wv=wv.astype(bf),
        w_a=w_a.astype(bf), w_b=w_b.astype(bf), w_o=w_o.astype(bf), post_g=post_g[None, :])


def _rope_tables(s):
    half = MLA_ROPE // 2
    freqs = ROPE_THETA ** (-jnp.arange(half, dtype=jnp.float32) / half)
    ang = jnp.arange(s).astype(jnp.float32)[:, None] * freqs[None, :]
    cos = jnp.cos(ang)
    sin = jnp.sin(ang)
    return jnp.concatenate([cos, cos], axis=1), jnp.concatenate([sin, sin], axis=1)


def _tiles(b, s):
    tk = min(512, s)
    return dict(ts=min(512, s), ts_up=min(1024, s), tk=tk, tq=min(2048, s),
                attn_unroll=min(8, s // tk))


def _layer(x, w):
    b, s, _ = x.shape
    t = b * s
    tl = _tiles(b, s)
    x2d = x.reshape(t, D_MODEL)
    qkv, lat, act, gates, gmin = _proj(x2d, w["pre_g"], w["wqkv"], w["wlat"], w["wact"], w["wg"],
                                       w["bg"], tl["ts"])
    lat3 = lat.reshape(b, s, LAT_GR)
    cos, sin = _rope_tables(s)
    qt, kf, vt = _mla_up(lat3, cos, sin, w["q_g"], w["kv_g"], w["wqn"], w["wqr"], w["wqrot"],
                         w["wkn"], w["wv"], tl["ts_up"], tl["tk"])
    y_b = _attn(qt, kf, vt, act.reshape(b, s, ACT_W), tl["tq"], tl["attn_unroll"])
    o_f, o_b = _gla(qkv.reshape(b, s, QKV_W), gates.reshape(b, s, GATES_W),
                    gmin.reshape(b, s // tl["ts"], 8, V7X_LANES), tl["ts"])
    y = _final(x2d, o_f.reshape(t, GLA_V_W), o_b.reshape(t, GLA_V_W), y_b.reshape(t, MLA_V_W),
               act, w["gla_g"], w["post_g"], w["w_a"], w["w_b"], w["w_o"], tl["ts"])
    return y.reshape(b, s, D_MODEL)


def kernel(x_prompt, x_sample, pre_norm_g, w_in, gla_wg2_fwd, gla_bg2_fwd, gla_wg2_bwd, gla_bg2_bwd,
           gla_out_norm_g, mla_q_norm_g, mla_w_uq, mla_kv_norm_g, mla_w_ukv, w_branch_a, w_branch_b,
           w_out, post_norm_g):
    depth = pre_norm_g.shape[0]
    y_prompt, y_sample = x_prompt, x_sample
    for l in range(depth):
        w = _prepare_weights(pre_norm_g[l], w_in[l], gla_wg2_fwd[l], gla_bg2_fwd[l], gla_wg2_bwd[l],
                             gla_bg2_bwd[l], gla_out_norm_g[l], mla_q_norm_g[l], mla_w_uq[l],
                             mla_kv_norm_g[l], mla_w_ukv[l], w_branch_a[l], w_branch_b[l],
                             w_out[l], post_norm_g[l])
        y_prompt = _layer(y_prompt, w)
        y_sample = _layer(y_sample, w)
    return (y_prompt, y_sample)
```

```python
import functools
import math

import jax
import jax.numpy as jnp
from jax import lax
from jax.experimental import pallas as pl
from jax.experimental.pallas import tpu as pltpu

D_MODEL = 1024
GLA_HEADS = 4
GLA_DK = 128
GLA_DV = 256
GLA_GATE_RANK = 16
GLA_GATE_NORM = 16.0
MLA_HEADS = 8
MLA_Q_RANK = 256
MLA_KV_RANK = 128
MLA_NOPE = 128
MLA_ROPE = 64
MLA_V = 128
ROPE_THETA = 10000.0
EPS = 1e-6

GLA_QK_W = GLA_HEADS * GLA_DK
GLA_V_W = GLA_HEADS * GLA_DV
MLA_V_W = MLA_HEADS * MLA_V
MLA_QK = MLA_NOPE + MLA_ROPE

V7X_LANES = 128
V7X_VMEM_LIMIT_BYTES = 56 * 1024 * 1024

QKV_W = 2 * GLA_QK_W + GLA_V_W
LAT_CQ = 0
LAT_CKV = MLA_Q_RANK
LAT_KR = LAT_CKV + MLA_KV_RANK
LAT_KROT = LAT_KR + V7X_LANES
LAT_GR = LAT_KROT + V7X_LANES
LAT_W = LAT_GR + V7X_LANES
ACT_W = 4 * D_MODEL
GATES_W = 2 * GLA_QK_W

GLA_KERNEL_CHUNK = 128
GLA_LOOKAHEAD = 4
GLA_EXP2_CLAMP = 115.0
LOG2E = 1.4426950408889634
ATTN_QSUB = 256
ATTN_LOOKAHEAD = 3
ATTN_DENOM_MIN = 2.0 ** -60
ATTN_DENOM_MAX = 2.0 ** 60


def _cparams(semantics):
    return pltpu.CompilerParams(dimension_semantics=semantics,
                                vmem_limit_bytes=V7X_VMEM_LIMIT_BYTES)


def _const_spec(shape):
    return pl.BlockSpec(shape, lambda *_: (0,) * len(shape))


def _rms(x, g):
    return x * lax.rsqrt(jnp.mean(x * x, axis=-1, keepdims=True) + EPS) * g


def _sigmoid(x):
    return 1.0 / (1.0 + jnp.exp(-x))


def _bdot(a, b):
    return jnp.dot(a, b, preferred_element_type=jnp.float32)


def _bdot_nt(a, b):
    return lax.dot_general(a, b, (((1,), (1,)), ((), ())), preferred_element_type=jnp.float32)


def _bdot_tn(a, b):
    return lax.dot_general(a, b, (((0,), (0,)), ((), ())), preferred_element_type=jnp.float32)


def _proj_kernel(x_ref, g_ref, wqkv_ref, wlat_ref, wact_ref, wg_ref, bg_ref,
                 qkv_ref, lat_ref, act_ref, gates_ref, gmin_ref):
    half = x_ref.shape[0] // 2
    gate_mins = []
    for r in (slice(0, half), slice(half, 2 * half)):
        h = _rms(x_ref[r, :], g_ref[...]).astype(jnp.bfloat16)
        lat = _bdot(h, wlat_ref[...])
        xg = _bdot(lat[:, LAT_GR:].astype(jnp.bfloat16), wg_ref[...]) + bg_ref[...]
        qkv = _bdot(h, wqkv_ref[...])
        lat_ref[r, :] = lat[:, :LAT_GR]
        gates = ((jnp.minimum(xg, 0.0) - jnp.log(1.0 + jnp.exp(-jnp.abs(xg))))
                 * (LOG2E / GLA_GATE_NORM))
        gates_ref[r, :] = gates
        by_sublane = jnp.min(gates.reshape(half // 8, 8, GATES_W), axis=0)
        gate_mins += [by_sublane[:, i * V7X_LANES:(i + 1) * V7X_LANES]
                      for i in range(GATES_W // V7X_LANES)]
        z = _bdot(h, wact_ref[:, :2 * D_MODEL])
        qkv_ref[r, :GLA_QK_W] = (qkv[:, :GLA_QK_W] * (GLA_DK ** -0.5)).astype(jnp.bfloat16)
        qkv_ref[r, GLA_QK_W:] = qkv[:, GLA_QK_W:].astype(jnp.bfloat16)
        gate = _bdot(h, wact_ref[:, 2 * D_MODEL:])
        act_ref[r, :2 * D_MODEL] = (z * _sigmoid(z)).astype(jnp.bfloat16)
        act_ref[r, 2 * D_MODEL:] = _sigmoid(gate).astype(jnp.bfloat16)
    gmin_ref[...] = functools.reduce(jnp.minimum, gate_mins)


def _proj(x2d, pre_g, wqkv, wlat, wact, wg, bg, ts):
    t = x2d.shape[0]
    return pl.pallas_call(
        _proj_kernel,
        grid=(t // ts,),
        in_specs=[pl.BlockSpec((ts, D_MODEL), lambda i: (i, 0)),
                  _const_spec((1, D_MODEL)),
                  _const_spec((D_MODEL, QKV_W)),
                  _const_spec((D_MODEL, LAT_W)),
                  _const_spec((D_MODEL, ACT_W)),
                  _const_spec((V7X_LANES, GATES_W)),
                  _const_spec((1, GATES_W))],
        out_specs=[pl.BlockSpec((ts, QKV_W), lambda i: (i, 0)),
                   pl.BlockSpec((ts, LAT_GR), lambda i: (i, 0)),
                   pl.BlockSpec((ts, ACT_W), lambda i: (i, 0)),
                   pl.BlockSpec((ts, GATES_W), lambda i: (i, 0)),
                   pl.BlockSpec((None, 8, V7X_LANES), lambda i: (i, 0, 0))],
        out_shape=[jax.ShapeDtypeStruct((t, QKV_W), jnp.bfloat16),
                   jax.ShapeDtypeStruct((t, LAT_GR), jnp.float32),
                   jax.ShapeDtypeStruct((t, ACT_W), jnp.bfloat16),
                   jax.ShapeDtypeStruct((t, GATES_W), jnp.float32),
                   jax.ShapeDtypeStruct((t // ts, 8, V7X_LANES), jnp.float32)],
        compiler_params=_cparams(("parallel",)),
        name="proj",
    )(x2d, pre_g, wqkv, wlat, wact, wg, bg)


def _mla_up_kernel(lat_ref, cos_ref, sin_ref, qg_ref, kvg_ref, wqn_ref, wqr_ref, wqrot_ref,
                   wkn_ref, wv_ref, qt_ref, kf_ref, vt_ref):
    lat = lat_ref[...]
    cos = cos_ref[...]
    sin = sin_ref[...]
    c_q = _rms(lat[:, LAT_CQ:LAT_CQ + MLA_Q_RANK], qg_ref[...]).astype(jnp.bfloat16)
    c_kv = _rms(lat[:, LAT_CKV:LAT_CKV + MLA_KV_RANK], kvg_ref[...]).astype(jnp.bfloat16)

    q_scale = (MLA_QK ** -0.5) * LOG2E
    qn_t = (_bdot(c_q, wqn_ref[...]) * q_scale).T
    cos_h = jnp.concatenate([cos, cos], axis=1)
    sin_h = jnp.concatenate([sin, sin], axis=1)
    cos_q = jnp.concatenate([cos_h] * (MLA_HEADS // 2), axis=1)
    sin_q = jnp.concatenate([sin_h] * (MLA_HEADS // 2), axis=1)
    q_pe = _bdot(c_q, wqr_ref[...]) * cos_q + _bdot(c_q, wqrot_ref[...]) * sin_q
    qpe_t = (q_pe * q_scale).T

    k_pe = (lat[:, LAT_KR:LAT_KR + MLA_ROPE] * cos
            + lat[:, LAT_KROT:LAT_KROT + MLA_ROPE] * sin).astype(jnp.bfloat16)
    kn = _bdot(c_kv, wkn_ref[...]).astype(jnp.bfloat16)
    v_t = _bdot(c_kv, wv_ref[...]).T

    n_kc, tk = vt_ref.shape[1], vt_ref.shape[3]
    for h in range(MLA_HEADS):
        qt_ref[h, :MLA_NOPE, :] = qn_t[h * MLA_NOPE:(h + 1) * MLA_NOPE, :].astype(jnp.bfloat16)
        qt_ref[h, MLA_NOPE:, :] = qpe_t[h * MLA_ROPE:(h + 1) * MLA_ROPE, :].astype(jnp.bfloat16)
        kf_ref[h, :, :MLA_NOPE] = kn[:, h * MLA_NOPE:(h + 1) * MLA_NOPE]
        kf_ref[h, :, MLA_NOPE:] = k_pe
        for u in range(n_kc):
            vt_ref[h, u] = v_t[h * MLA_V:(h + 1) * MLA_V, u * tk:(u + 1) * tk].astype(jnp.bfloat16)


def _mla_up(lat3, cos, sin, qg, kvg, wqn, wqr, wqrot, wkn, wv, ts, tk):
    b, s, _ = lat3.shape
    hh = MLA_HEADS
    return pl.pallas_call(
        _mla_up_kernel,
        grid=(b, s // ts),
        in_specs=[pl.BlockSpec((None, ts, LAT_GR), lambda i, j: (i, j, 0)),
                  pl.BlockSpec((ts, MLA_ROPE), lambda i, j: (j, 0)),
                  pl.BlockSpec((ts, MLA_ROPE), lambda i, j: (j, 0)),
                  _const_spec((1, MLA_Q_RANK)),
                  _const_spec((1, MLA_KV_RANK)),
                  _const_spec((MLA_Q_RANK, hh * MLA_NOPE)),
                  _const_spec((MLA_Q_RANK, hh * MLA_ROPE)),
                  _const_spec((MLA_Q_RANK, hh * MLA_ROPE)),
                  _const_spec((MLA_KV_RANK, hh * MLA_NOPE)),
                  _const_spec((MLA_KV_RANK, hh * MLA_V))],
        out_specs=[pl.BlockSpec((None, hh, MLA_QK, ts), lambda i, j: (i, 0, 0, j)),
                   pl.BlockSpec((None, hh, ts, MLA_QK), lambda i, j: (i, 0, j, 0)),
                   pl.BlockSpec((None, hh, ts // tk, MLA_V, tk), lambda i, j: (i, 0, j, 0, 0))],
        out_shape=[jax.ShapeDtypeStruct((b, hh, MLA_QK, s), jnp.bfloat16),
                   jax.ShapeDtypeStruct((b, hh, s, MLA_QK), jnp.bfloat16),
                   jax.ShapeDtypeStruct((b, hh, s // tk, MLA_V, tk), jnp.bfloat16)],
        compiler_params=_cparams(("parallel", "parallel")),
        name="mla_up",
    )(lat3, cos, sin, qg, kvg, wqn, wqr, wqrot, wkn, wv)


def _attn_kernel(qt_ref, k_ref, vt_ref, o_ref, m_ref, l_ref, acc_ref, *, unroll):
    n_chunks, _, tk = vt_ref.shape
    n_sub = qt_ref.shape[1] // ATTN_QSUB

    def scores(c, j):
        rows = pl.ds(pl.multiple_of(c * tk, tk), tk)
        return _bdot(k_ref[rows, :], qt_ref[:, j * ATTN_QSUB:(j + 1) * ATTN_QSUB])

    def run(chain_tail):
        acc_ref[...] = jnp.zeros_like(acc_ref)
        l_ref[...] = jnp.zeros_like(l_ref)

        def group(g, carry):
            chains = [(g * unroll + u, j) for u in range(unroll) for j in range(n_sub)]
            pending = [scores(*ch) for ch in chains[:ATTN_LOOKAHEAD]]
            for idx, (c, j) in enumerate(chains):
                s = pending.pop(0)
                if idx + ATTN_LOOKAHEAD < len(chains):
                    pending.append(scores(*chains[idx + ATTN_LOOKAHEAD]))
                chain_tail(s, c, slice(j * ATTN_QSUB, (j + 1) * ATTN_QSUB))
            return carry

        n_groups = n_chunks // unroll
        if n_groups == 1:
            group(0, 0)
        else:
            lax.fori_loop(0, n_groups, group, 0)

    def plain_tail(s, c, cols):
        p = jnp.exp2(s)
        l_ref[:, cols] = l_ref[:, cols] + jnp.sum(p, axis=0, keepdims=True)
        acc_ref[:, cols] = acc_ref[:, cols] + _bdot(vt_ref[c], p.astype(jnp.bfloat16))

    def online_tail(s, c, cols):
        m_old = m_ref[:, cols]
        m_new = jnp.maximum(m_old, jnp.max(s, axis=0, keepdims=True))
        alpha = jnp.exp2(m_old - m_new)
        p = jnp.exp2(s - m_new)
        l_ref[:, cols] = alpha * l_ref[:, cols] + jnp.sum(p, axis=0, keepdims=True)
        acc_ref[:, cols] = alpha * acc_ref[:, cols] + _bdot(vt_ref[c], p.astype(jnp.bfloat16))
        m_ref[:, cols] = m_new

    run(plain_tail)
    denom = l_ref[...]
    in_range = (denom >= ATTN_DENOM_MIN) & (denom <= ATTN_DENOM_MAX)
    n_bad = jnp.sum(jnp.where(in_range, 0.0, 1.0))

    @pl.when(n_bad > 0.0)
    def _():
        m_ref[...] = jnp.full_like(m_ref, -jnp.inf)
        run(online_tail)

    o_ref[...] = (acc_ref[...] / l_ref[...]).T.astype(o_ref.dtype)


def _attn(qt, kf, vt, tq, unroll):
    b, hh, _, s = qt.shape
    n_chunks, _, tk = vt.shape[2:]
    return pl.pallas_call(
        functools.partial(_attn_kernel, unroll=unroll),
        grid=(b, hh, s // tq),
        in_specs=[pl.BlockSpec((None, None, MLA_QK, tq), lambda i, h, q: (i, h, 0, q)),
                  pl.BlockSpec((None, None, s, MLA_QK), lambda i, h, q: (i, h, 0, 0)),
                  pl.BlockSpec((None, None, n_chunks, MLA_V, tk),
                               lambda i, h, q: (i, h, 0, 0, 0))],
        out_specs=pl.BlockSpec((None, tq, MLA_V), lambda i, h, q: (i, q, h)),
        out_shape=jax.ShapeDtypeStruct((b, s, MLA_V_W), jnp.bfloat16),
        scratch_shapes=[pltpu.VMEM((1, tq), jnp.float32),
                        pltpu.VMEM((1, tq), jnp.float32),
                        pltpu.VMEM((MLA_V, tq), jnp.float32)],
        compiler_params=_cparams(("parallel", "parallel", "parallel")),
        name="attn",
    )(qt, kf, vt)


def _split_bf16(x, pieces):
    out = []
    for _ in range(pieces):
        term = x.astype(jnp.bfloat16)
        out.append(term)
        x = x - term.astype(jnp.float32)
    return out


def _gla_cumsum(g, reverse, pieces):
    c = GLA_KERNEL_CHUNK
    row = lax.broadcasted_iota(jnp.int32, (c, c), 0)
    col = lax.broadcasted_iota(jnp.int32, (c, c), 1)
    tri = ((col >= row) if reverse else (col <= row)).astype(jnp.bfloat16)
    return _bdot(jnp.concatenate([tri] * pieces, axis=1),
                 jnp.concatenate(_split_bf16(g, pieces), axis=0))


def _gla_scores_factorised(q, k, b, keep):
    c = GLA_KERNEL_CHUNK
    d = b - b[c // 2:c // 2 + 1, :]
    q_in = (q * jnp.exp2(jnp.minimum(d, GLA_EXP2_CLAMP))).astype(jnp.bfloat16)
    k_in = (k * jnp.exp2(jnp.minimum(-d, GLA_EXP2_CLAMP))).astype(jnp.bfloat16)
    zeros = jnp.zeros((c, GLA_DK), jnp.bfloat16)
    keep2 = jnp.concatenate([keep, keep], axis=1)
    scores = []
    for hp in range(GLA_HEADS // 2):
        k0 = k_in[:, (2 * hp) * GLA_DK:(2 * hp + 1) * GLA_DK]
        k1 = k_in[:, (2 * hp + 1) * GLA_DK:(2 * hp + 2) * GLA_DK]
        k_diag = jnp.concatenate([jnp.concatenate([k0, zeros], axis=1),
                                  jnp.concatenate([zeros, k1], axis=1)], axis=0)
        a2 = _bdot_nt(q_in[:, 2 * hp * GLA_DK:(2 * hp + 2) * GLA_DK], k_diag)
        a2 = jnp.where(keep2, a2, 0.0).astype(jnp.bfloat16)
        scores += [a2[:, :c], a2[:, c:]]
    return scores


def _gla_scores_direct(q, k, b, keep, b_scr, k_scr):
    c = GLA_KERNEL_CHUNK
    b_scr[...] = b
    k_scr[...] = k
    col = lax.broadcasted_iota(jnp.int32, (c, c), 1)

    def key_row(s, accs):
        e = jnp.exp2(jnp.minimum(b - b_scr[pl.ds(s, 1), :], 0.0)) * (q * k_scr[pl.ds(s, 1), :])
        return tuple(
            jnp.where(col == s,
                      jnp.sum(e[:, h * GLA_DK:(h + 1) * GLA_DK], axis=1, keepdims=True), acc)
            for h, acc in enumerate(accs))

    accs = lax.fori_loop(0, c, key_row,
                         tuple(jnp.zeros((c, c), jnp.float32) for _ in range(GLA_HEADS)))
    return [jnp.where(keep, a, 0.0).astype(jnp.bfloat16) for a in accs]


def _gla_chunk_local(qkv_ref, b, rows, reverse, direct_scratch):
    c = GLA_KERNEL_CHUNK
    row = lax.broadcasted_iota(jnp.int32, (c, c), 0)
    col = lax.broadcasted_iota(jnp.int32, (c, c), 1)
    keep = (col > row) if reverse else (col <= row)
    last = 0 if reverse else c - 1
    b_last = b[last:last + 1, :]
    q = qkv_ref[rows, 0:GLA_QK_W].astype(jnp.float32)
    k = qkv_ref[rows, GLA_QK_W:2 * GLA_QK_W].astype(jnp.float32)
    q_st = (q * jnp.exp2(b)).astype(jnp.bfloat16)
    k_st = (k * jnp.exp2(b_last - b)).astype(jnp.bfloat16)
    e_last = jnp.exp2(b_last)
    if direct_scratch is None:
        scores = _gla_scores_factorised(q, k, b, keep)
    else:
        scores = _gla_scores_direct(q, k, b, keep, *direct_scratch)
    heads = []
    for h in range(GLA_HEADS):
        ks = slice(h * GLA_DK, (h + 1) * GLA_DK)
        v = qkv_ref[rows, 2 * GLA_QK_W + h * GLA_DV:2 * GLA_QK_W + (h + 1) * GLA_DV]
        upd = _bdot_tn(k_st[:, ks], v)
        decay_col = jnp.broadcast_to(e_last[:, ks], (GLA_DK, GLA_DK)).T
        decay = jnp.concatenate([decay_col] * (GLA_DV // GLA_DK), axis=1)
        heads.append((jnp.concatenate([scores[h], q_st[:, ks]], axis=1), v, upd, decay))
    return heads


def _gla_chunk_state(heads, rows, o_ref, state_ref):
    for h, (lhs, v, upd, decay) in enumerate(heads):
        st = state_ref[h]
        o = _bdot(lhs, jnp.concatenate([v, st.astype(jnp.bfloat16)], axis=0))
        o_ref[rows, h * GLA_DV:(h + 1) * GLA_DV] = o.astype(o_ref.dtype)
        state_ref[h] = st * decay + upd


def _gla_block(qkv_f_ref, g_f_ref, qkv_b_ref, g_b_ref, of_ref, ob_ref, sf_ref, sb_ref,
               direct_scratch):
    c = GLA_KERNEL_CHUNK
    n_chunks = qkv_f_ref.shape[0] // c
    work = []
    for ci in range(n_chunks):
        rows_f = slice(ci * c, (ci + 1) * c)
        rows_b = slice((n_chunks - 1 - ci) * c, (n_chunks - ci) * c)
        work.append((qkv_f_ref, g_f_ref, rows_f, False, of_ref, sf_ref))
        work.append((qkv_b_ref, g_b_ref, rows_b, True, ob_ref, sb_ref))
    pieces = 2 if direct_scratch is None else 3
    sums = [_gla_cumsum(g_ref[rows, :], rev, pieces) for (_, g_ref, rows, rev, _, _) in work]
    local = lambda i: _gla_chunk_local(work[i][0], sums[i], work[i][2], work[i][3], direct_scratch)
    pending = [local(i) for i in range(min(GLA_LOOKAHEAD, len(work)))]
    for i, (_, _, rows, _, o_ref, state_ref) in enumerate(work):
        heads = pending.pop(0)
        if i + GLA_LOOKAHEAD < len(work):
            pending.append(local(i + GLA_LOOKAHEAD))
        _gla_chunk_state(heads, rows, o_ref, state_ref)


def _gla_kernel(qkv_f_ref, g_f_ref, gmin_f_ref, qkv_b_ref, g_b_ref, gmin_b_ref,
                of_ref, ob_ref, sf_ref, sb_ref, b_scr, k_scr):
    @pl.when(pl.program_id(1) == 0)
    def _():
        sf_ref[...] = jnp.zeros_like(sf_ref)
        sb_ref[...] = jnp.zeros_like(sb_ref)

    refs = (qkv_f_ref, g_f_ref, qkv_b_ref, g_b_ref, of_ref, ob_ref, sf_ref, sb_ref)
    g_min = jnp.min(jnp.minimum(gmin_f_ref[...], gmin_b_ref[...]))
    steep = g_min * (GLA_KERNEL_CHUNK // 2) < -GLA_EXP2_CLAMP
    lax.cond(steep,
             lambda: _gla_block(*refs, (b_scr, k_scr)),
             lambda: _gla_block(*refs, None))


def _gla(qkv3, gates3, gmin4, tb):
    b, s, _ = qkv3.shape
    nb = s // tb
    assert gmin4.shape[:2] == (b, nb)
    fwd = lambda i, j: (i, j, 0)
    bwd = lambda i, j: (i, nb - 1 - j, 0)
    gmin_spec = lambda index_map: pl.BlockSpec((None, None, 8, V7X_LANES), index_map)
    return pl.pallas_call(
        _gla_kernel,
        grid=(b, nb),
        in_specs=[pl.BlockSpec((None, tb, QKV_W), fwd),
                  pl.BlockSpec((None, tb, GLA_QK_W), fwd),
                  gmin_spec(lambda i, j: (i, j, 0, 0)),
                  pl.BlockSpec((None, tb, QKV_W), bwd),
                  pl.BlockSpec((None, tb, GLA_QK_W), lambda i, j: (i, nb - 1 - j, 1)),
                  gmin_spec(lambda i, j: (i, nb - 1 - j, 0, 0))],
        out_specs=[pl.BlockSpec((None, tb, GLA_V_W), fwd),
                   pl.BlockSpec((None, tb, GLA_V_W), bwd)],
        out_shape=[jax.ShapeDtypeStruct((b, s, GLA_V_W), jnp.bfloat16),
                   jax.ShapeDtypeStruct((b, s, GLA_V_W), jnp.bfloat16)],
        scratch_shapes=[pltpu.VMEM((GLA_HEADS, GLA_DK, GLA_DV), jnp.float32),
                        pltpu.VMEM((GLA_HEADS, GLA_DK, GLA_DV), jnp.float32),
                        pltpu.VMEM((GLA_KERNEL_CHUNK, GLA_QK_W), jnp.float32),
                        pltpu.VMEM((GLA_KERNEL_CHUNK, GLA_QK_W), jnp.float32)],
        compiler_params=_cparams(("parallel", "arbitrary")),
        name="gla",
    )(qkv3, gates3, gmin4, qkv3, gates3, gmin4)


def _final_kernel(x_ref, of_ref, ob_ref, om_ref, act_ref, gg_ref, pg_ref, wa_ref, wb_ref, wo_ref,
                  y_ref):
    gg = gg_ref[...]
    half = x_ref.shape[0] // 2
    rs = (slice(0, half), slice(half, 2 * half))
    pre = []
    for r in rs:
        o_sum = of_ref[r, :].astype(jnp.float32) + ob_ref[r, :].astype(jnp.float32)
        o_a = jnp.concatenate(
            [_rms(o_sum[:, h * GLA_DV:(h + 1) * GLA_DV], gg) for h in range(GLA_HEADS)], axis=1)
        y_a = (o_a * act_ref[r, 0:D_MODEL].astype(jnp.float32)).astype(jnp.bfloat16)
        y_b = om_ref[r, :] * act_ref[r, D_MODEL:2 * D_MODEL]
        pre.append((y_a, y_b))
    branch = [(_bdot(y_a, wa_ref[...]), _bdot(y_b, wb_ref[...])) for (y_a, y_b) in pre]
    outs = []
    for r, (t_a, t_b) in zip(rs, branch):
        merged = (act_ref[r, 2 * D_MODEL:3 * D_MODEL].astype(jnp.float32) * t_a
                  + act_ref[r, 3 * D_MODEL:].astype(jnp.float32) * t_b)
        outs.append(_bdot(merged.astype(jnp.bfloat16), wo_ref[...]))
    for r, out in zip(rs, outs):
        y_ref[r, :] = x_ref[r, :] + _rms(out, pg_ref[...])


def _final(x2d, o_f, o_b, o_m, act, gg, pg, wa, wb, wo, ts):
    t = x2d.shape[0]
    tok = lambda w: pl.BlockSpec((ts, w), lambda i: (i, 0))
    return pl.pallas_call(
        _final_kernel,
        grid=(t // ts,),
        in_specs=[tok(D_MODEL), tok(GLA_V_W), tok(GLA_V_W), tok(MLA_V_W), tok(ACT_W),
                  _const_spec((1, GLA_DV)), _const_spec((1, D_MODEL)),
                  _const_spec((GLA_V_W, D_MODEL)), _const_spec((MLA_V_W, D_MODEL)),
                  _const_spec((D_MODEL, D_MODEL))],
        out_specs=tok(D_MODEL),
        out_shape=jax.ShapeDtypeStruct((t, D_MODEL), jnp.float32),
        compiler_params=_cparams(("parallel",)),
        name="final",
    )(x2d, o_f, o_b, o_m, act, gg, pg, wa, wb, wo)


def _rotate_half_cols(w):
    half = MLA_ROPE // 2
    w3 = w.reshape(w.shape[0], -1, MLA_ROPE)
    return jnp.concatenate([-w3[..., half:], w3[..., :half]], axis=-1).reshape(w.shape)


def _pad_cols(w, width):
    return jnp.pad(w, ((0, 0), (0, width - w.shape[1])))


def _prepare_weights(pre_g, w_in, wg2_f, bg2_f, wg2_b, bg2_b, gla_g, q_g, w_uq, kv_g, w_ukv,
                     w_a, w_b, w_o, post_g):
    bf = jnp.bfloat16
    sizes = (GLA_QK_W, GLA_QK_W, GLA_V_W, GLA_GATE_RANK, GLA_GATE_RANK, GLA_V_W,
             MLA_Q_RANK, MLA_KV_RANK, MLA_ROPE, MLA_V_W, D_MODEL, D_MODEL)
    offs = [0]
    for sz in sizes:
        offs.append(offs[-1] + sz)
    (w_q, w_k, w_v, w_grf, w_grb, w_za, w_cq, w_ckv, w_kr, w_zb, w_ga, w_gb) = [
        w_in[:, offs[i]:offs[i + 1]] for i in range(len(sizes))]
    wqkv = jnp.concatenate([w_q, w_k, w_v], axis=1).astype(bf)
    wlat = jnp.concatenate([
        w_cq, w_ckv, _pad_cols(w_kr, V7X_LANES), _pad_cols(_rotate_half_cols(w_kr), V7X_LANES),
        _pad_cols(jnp.concatenate([w_grf, w_grb], axis=1), V7X_LANES)], axis=1).astype(bf)
    wact = jnp.concatenate([w_za, w_zb, w_ga, w_gb], axis=1).astype(bf)

    wg = jnp.zeros((V7X_LANES, GATES_W), jnp.float32)
    wg = wg.at[0:GLA_GATE_RANK, :GLA_QK_W].set(wg2_f)
    wg = wg.at[GLA_GATE_RANK:2 * GLA_GATE_RANK, GLA_QK_W:].set(wg2_b).astype(bf)
    bg = jnp.concatenate([bg2_f, bg2_b])[None, :]

    uq = w_uq.reshape(MLA_Q_RANK, MLA_HEADS, MLA_QK)
    wqn = uq[:, :, :MLA_NOPE].reshape(MLA_Q_RANK, MLA_HEADS * MLA_NOPE)
    wqr = uq[:, :, MLA_NOPE:].reshape(MLA_Q_RANK, MLA_HEADS * MLA_ROPE)
    ukv = w_ukv.reshape(MLA_KV_RANK, MLA_HEADS, MLA_NOPE + MLA_V)
    wkn = ukv[:, :, :MLA_NOPE].reshape(MLA_KV_RANK, MLA_HEADS * MLA_NOPE)
    wv = ukv[:, :, MLA_NOPE:].reshape(MLA_KV_RANK, MLA_HEADS * MLA_V)
    return dict(
        pre_g=pre_g[None, :], wqkv=wqkv, wlat=wlat, wact=wact,
        wg=wg, bg=bg,
        gla_g=gla_g[None, :], q_g=q_g[None, :], kv_g=kv_g[None, :],
        wqn=wqn.astype(bf), wqr=wqr.astype(bf), wqrot=_rotate_half_cols(wqr).astype(bf),
        wkn=wkn.astype(bf), wv=wv.astype(bf),
        w_a=w_a.astype(bf), w_b=w_b.astype(bf), w_o=w_o.astype(bf), post_g=post_g[None, :])


def _rope_tables(s):
    half = MLA_ROPE // 2
    freqs = ROPE_THETA ** (-jnp.arange(half, dtype=jnp.float32) / half)
    ang = jnp.arange(s).astype(jnp.float32)[:, None] * freqs[None, :]
    cos = jnp.cos(ang)
    sin = jnp.sin(ang)
    return jnp.concatenate([cos, cos], axis=1), jnp.concatenate([sin, sin], axis=1)


def _tiles(b, s):
    tk = min(512, s)
    return dict(ts=min(512, s), ts_up=min(1024, s), tk=tk, tq=min(2048, s),
                attn_unroll=min(8, s // tk))


def _layer(x, w):
    b, s, _ = x.shape
    t = b * s
    tl = _tiles(b, s)
    x2d = x.reshape(t, D_MODEL)
    qkv, lat, act, gates, gmin = _proj(x2d, w["pre_g"], w["wqkv"], w["wlat"], w["wact"], w["wg"],
                                       w["bg"], tl["ts"])
    lat3 = lat.reshape(b, s, LAT_GR)
    cos, sin = _rope_tables(s)
    qt, kf, vt = _mla_up(lat3, cos, sin, w["q_g"], w["kv_g"], w["wqn"], w["wqr"], w["wqrot"],
                         w["wkn"], w["wv"], tl["ts_up"], tl["tk"])
    o_m = _attn(qt, kf, vt, tl["tq"], tl["attn_unroll"])
    o_f, o_b = _gla(qkv.reshape(b, s, QKV_W), gates.reshape(b, s, GATES_W),
                    gmin.reshape(b, s // tl["ts"], 8, V7X_LANES), tl["ts"])
    y = _final(x2d, o_f.reshape(t, GLA_V_W), o_b.reshape(t, GLA_V_W), o_m.reshape(t, MLA_V_W),
               act, w["gla_g"], w["post_g"], w["w_a"], w["w_b"], w["w_o"], tl["ts"])
    return y.reshape(b, s, D_MODEL)


def kernel(x_prompt, x_sample, pre_norm_g, w_in, gla_wg2_fwd, gla_bg2_fwd, gla_wg2_bwd, gla_bg2_bwd,
           gla_out_norm_g, mla_q_norm_g, mla_w_uq, mla_kv_norm_g, mla_w_ukv, w_branch_a, w_branch_b,
           w_out, post_norm_g):
    depth = pre_norm_g.shape[0]
    y_prompt, y_sample = x_prompt, x_sample
    for l in range(depth):
        w = _prepare_weights(pre_norm_g[l], w_in[l], gla_wg2_fwd[l], gla_bg2_fwd[l], gla_wg2_bwd[l],
                             gla_bg2_bwd[l], gla_out_norm_g[l], mla_q_norm_g[l], mla_w_uq[l],
                             mla_kv_norm_g[l], mla_w_ukv[l], w_branch_a[l], w_branch_b[l],
                             w_out[l], post_norm_g[l])
        y_prompt = _layer(y_prompt, w)
        y_sample = _layer(y_sample, w)
    return (y_prompt, y_sample)
```
